```python
import jax
import jax.numpy as jnp
from jax import lax

D_MODEL = 1024
BATCH = 2
SEQ = 8192
DEPTH = 4
DEC_BATCH = 32
DEC_SEQ = 4
PAST_LEN = 8192
PAGE_SIZE = 128

H_RET = 4
DK_RET = 128
DV_RET = 128
W_RET = H_RET * DK_RET
RET_CHUNK = 128
ROPE_BASE = 10000.0
G_SGU = 4
C_SGU = 128
W_SGU = G_SGU * C_SGU
SGU_CHUNK = 128
H_SB = 4
D_SB = 128
W_SB = H_SB * D_SB
Q_BLOCK = 128
SB_BIAS_INIT = -5.0
N_BRANCH = 3
BRANCH_W = 512
IN_WIDTHS = (W_RET, W_RET, W_RET, W_RET, W_SGU, W_SGU, W_SB, W_SB, W_SB, D_MODEL, D_MODEL, D_MODEL)
W_IN = sum(IN_WIDTHS)
PEER_HEADS = 8
PEER_NKEYS = 128
PEER_EXPERTS = PEER_NKEYS * PEER_NKEYS
PEER_DQ = 256
PEER_TOPK = 16
PEER_BLOCK = 128
EPS = 1e-6

kernel_name = 'hybrid_retention_gmlp_stickbreak_peer_step'


def _split_points():
    pts = []
    acc = 0
    for w in IN_WIDTHS[:-1]:
        acc += w
        pts.append(acc)
    return tuple(pts)


def rmsnorm(x, g):
    xf = x.astype(jnp.float32)
    y = xf * lax.rsqrt(jnp.mean(xf * xf, axis=-1, keepdims=True) + EPS)
    return (y * g.astype(jnp.float32)).astype(x.dtype)


def head_groupnorm(x, g):
    xf = x.astype(jnp.float32)
    mu = jnp.mean(xf, axis=-1, keepdims=True)
    var = jnp.mean(jnp.square(xf - mu), axis=-1, keepdims=True)
    y = ((xf - mu) * lax.rsqrt(var + EPS)).reshape(*x.shape[:-2], -1)
    return (y * g.astype(jnp.float32)).astype(x.dtype)


def rope(x, pos):
    half = x.shape[-1] // 2
    inv = ROPE_BASE ** (-jnp.arange(half, dtype=jnp.float32) / half)
    ang = pos.astype(jnp.float32)[:, None] * inv[None, :]
    cos = jnp.cos(ang)[None, :, None, :]
    sin = jnp.sin(ang)[None, :, None, :]
    x1, x2 = x[..., :half], x[..., half:]
    return jnp.concatenate([x1 * cos - x2 * sin, x1 * sin + x2 * cos], axis=-1).astype(x.dtype)


def retention(q, k, v, s0):
    B, L, H, _ = q.shape
    cs = RET_CHUNK if L % RET_CHUNK == 0 else L
    nc = L // cs
    log_g = jnp.log(1.0 - jnp.exp2(-5.0 - jnp.arange(H, dtype=jnp.float32)))
    i = jnp.arange(cs, dtype=jnp.float32)
    diff = i[:, None] - i[None, :]
    dmask = jnp.where(diff[None] >= 0, jnp.exp(diff[None] * log_g[:, None, None]), 0.0)
    k_dec = jnp.exp((cs - 1 - i)[:, None] * log_g[None, :])
    q_dec = jnp.exp((i + 1)[:, None] * log_g[None, :])
    chunk_dec = jnp.exp(cs * log_g)[None, :, None, None]
    qc = q.reshape(B, nc, cs, H, -1).astype(jnp.float32)
    kc = k.reshape(B, nc, cs, H, -1).astype(jnp.float32)
    vc = v.reshape(B, nc, cs, H, -1).astype(jnp.float32)
    scores = jnp.einsum('bnihd,bnjhd->bnhij', qc, kc) * dmask
    o = jnp.einsum('bnhij,bnjhe->bnihe', scores, vc)
    kv = jnp.einsum('bnjhd,bnjhe->nbhde', kc * k_dec[:, :, None], vc)

    def step(s, kv_n):
        return chunk_dec * s + kv_n, s

    s_fin, s_prev = lax.scan(step, s0.astype(jnp.float32), kv)
    o = o + jnp.einsum('bnihd,nbhde->bnihe', qc * q_dec[:, :, None], s_prev)
    return o.reshape(B, L, H, -1).astype(q.dtype), s_fin.astype(s0.dtype)


def spatial_gating(u, v, w_sp, b_sp, g):
    B, L, _ = v.shape
    vn = rmsnorm(v, g)
    nc = -(-L // SGU_CHUNK)
    vp = jnp.pad(vn, ((0, 0), (0, nc * SGU_CHUNK - L), (0, 0))).reshape(B, nc, SGU_CHUNK, G_SGU, C_SGU)
    tri = jnp.tril(jnp.ones((SGU_CHUNK, SGU_CHUNK), dtype=bool))
    w = jnp.where(tri[None], w_sp, 0.0)
    mixed = jnp.einsum('gts,bnsgc->bntgc', w, vp) + b_sp.T[None, None, :, :, None]
    mixed = mixed.reshape(B, nc * SGU_CHUNK, W_SGU)[:, :L]
    return u * mixed, vn


def stick_breaking(q, ks, vs, q_start, bias):
    B, Lq, H, D = q.shape
    qb = Q_BLOCK if Lq % Q_BLOCK == 0 else Lq
    nb = Lq // qb
    lens = [k_.shape[1] for k_ in ks]
    kpos = jnp.arange(sum(lens))
    scale = D ** -0.5
    bias_f = bias.astype(jnp.float32)[None, :, None, None]

    def block(n):
        qn = lax.dynamic_slice_in_dim(q, n * qb, qb, axis=1)
        qpos = q_start + n * qb + jnp.arange(qb)
        z = jnp.concatenate([jnp.einsum('bqhd,bkhd->bhqk', qn, k_) for k_ in ks], axis=-1).astype(jnp.float32) * scale + bias_f
        mask = kpos[None, :] < qpos[:, None]
        log_1m = jnp.where(mask, jax.nn.log_sigmoid(-z), 0.0)
        after = lax.cumsum(log_1m, axis=3, reverse=True) - log_1m
        a = jnp.where(mask, jnp.exp(jax.nn.log_sigmoid(z) + after), 0.0).astype(q.dtype)
        outs = []
        off = 0
        for v_, lk in zip(vs, lens):
            outs.append(jnp.einsum('bhqk,bkhd->bqhd', a[..., off:off + lk], v_))
            off += lk
        return sum(outs)

    o = lax.map(block, jnp.arange(nb))
    return jnp.moveaxis(o, 0, 1).reshape(B, Lq, H, D)


def peer(h, w_pq, sub_keys, exp_u, exp_v):
    lead = h.shape[:-1]
    hf = h.reshape(-1, D_MODEL)
    T = hf.shape[0]
    blk = min(PEER_BLOCK, T)
    nb = -(-T // blk)
    hp = jnp.pad(hf, ((0, nb * blk - T), (0, 0))).reshape(nb, blk, D_MODEL)

    def block(hb):
        q = (hb @ w_pq).reshape(blk, PEER_HEADS, 2, PEER_DQ // 2)
        s = jnp.einsum('thpd,pkd->thpk', q, sub_keys).astype(jnp.float32)
        s_top, i_top = lax.top_k(s, PEER_TOPK)
        cand = s_top[:, :, 0, :, None] + s_top[:, :, 1, None, :]
        cand_idx = i_top[:, :, 0, :, None] * PEER_NKEYS + i_top[:, :, 1, None, :]
        best, sel = lax.top_k(cand.reshape(blk, PEER_HEADS, -1), PEER_TOPK)
        eidx = jnp.take_along_axis(cand_idx.reshape(blk, PEER_HEADS, -1), sel, axis=-1)
        gate = jax.nn.softmax(best, axis=-1)
        act = jax.nn.gelu(jnp.einsum('td,thkd->thk', hb, exp_u[eidx]).astype(jnp.float32))
        return jnp.einsum('thk,thkd->td', (gate * act).astype(hb.dtype), exp_v[eidx])

    out = lax.map(block, hp).reshape(nb * blk, D_MODEL)[:T]
    return out.reshape(*lead, D_MODEL)


def trunk_layer(x, c, pos, s0, past_k, past_v, q_start,
                w_ada, b_ada, g_norm1, g_norm2, w_in, g_ret, g_q, g_k, b_sb, g_sgu, w_sp, b_sp,
                w_branch, w_out, w_pq, sub_keys, exp_u, exp_v):
    B, L, _ = x.shape
    mod = jax.nn.silu(c) @ w_ada + b_ada
    sh1, sc1, ga1, sh2, sc2, ga2 = [m[:, None, :] for m in jnp.split(mod, 6, axis=-1)]
    h = rmsnorm(x, g_norm1) * (1.0 + sc1) + sh1
    rq, rk, rv, rg, su, sv, bq, bk, bv, gA, gB, gC = jnp.split(h @ w_in, _split_points(), axis=-1)
    rq = rope(rq.reshape(B, L, H_RET, DK_RET), pos)
    rk = rope(rk.reshape(B, L, H_RET, DK_RET), pos) * DK_RET ** -0.5
    o_ret, s_new = retention(rq, rk, rv.reshape(B, L, H_RET, DV_RET), s0)
    o_ret = jax.nn.silu(rg) * head_groupnorm(o_ret, g_ret)
    o_sgu, v_rows = spatial_gating(su, sv, w_sp, b_sp, g_sgu)
    bq = rmsnorm(bq.reshape(B, L, H_SB, D_SB), g_q)
    bk = rmsnorm(bk.reshape(B, L, H_SB, D_SB), g_k)
    bv = bv.reshape(B, L, H_SB, D_SB)
    ks = (bk,) if past_k is None else (past_k, bk)
    vs = (bv,) if past_v is None else (past_v, bv)
    o_sb = stick_breaking(bq, ks, vs, q_start, b_sb).reshape(B, L, W_SB)
    merged = (jax.nn.sigmoid(gA) * (o_ret @ w_branch[0])
              + jax.nn.sigmoid(gB) * (o_sgu @ w_branch[1])
              + jax.nn.sigmoid(gC) * (o_sb @ w_branch[2]))
    x = x + ga1 * (merged @ w_out)
    h2 = rmsnorm(x, g_norm2) * (1.0 + sc2) + sh2
    x = x + ga2 * peer(h2, w_pq, sub_keys, exp_u, exp_v)
    return x, s_new, bk, bv, v_rows


def setup_inputs(seed: int = 0) -> dict:
    key = jax.random.key(seed)
    ks = jax.random.split(key, 26)
    n_pages = PAST_LEN // PAGE_SIZE
    n_used = DEC_BATCH * n_pages
    n_pool = (5 * n_used + 3) // 4

    def nrm(k, shape, s):
        return jax.random.normal(k, shape, jnp.float32) * s

    page_table = jax.random.permutation(ks[5], n_pool)[:n_used].reshape(DEC_BATCH, n_pages).astype(jnp.int32)
    return {
        'x_prompt': nrm(ks[0], (BATCH, SEQ, D_MODEL), 1.0),
        'x_sample': nrm(ks[1], (DEC_BATCH, DEC_SEQ, D_MODEL), 1.0),
        'cache_k': nrm(ks[2], (n_pool, DEPTH, PAGE_SIZE, H_SB, D_SB), 1.0),
        'cache_v': nrm(ks[3], (n_pool, DEPTH, PAGE_SIZE, H_SB, D_SB), 1.0),
        'state_ret': nrm(ks[4], (DEC_BATCH, DEPTH, H_RET, DK_RET, DV_RET), 1.0),
        'page_table': page_table,
        'c_prompt': nrm(ks[6], (BATCH, D_MODEL), 1.0),
        'c_sample': nrm(ks[7], (DEC_BATCH, D_MODEL), 1.0),
        'w_ada': nrm(ks[8], (DEPTH, D_MODEL, 6 * D_MODEL), 0.5 * D_MODEL ** -0.5),
        'b_ada': nrm(ks[9], (DEPTH, 6 * D_MODEL), 0.02),
        'g_norm1': 1.0 + nrm(ks[10], (DEPTH, D_MODEL), 0.02),
        'g_norm2': 1.0 + nrm(ks[11], (DEPTH, D_MODEL), 0.02),
        'w_in': nrm(ks[12], (DEPTH, D_MODEL, W_IN), D_MODEL ** -0.5),
        'g_ret': 1.0 + nrm(ks[13], (DEPTH, H_RET * DV_RET), 0.02),
        'g_q': 1.0 + nrm(ks[14], (DEPTH, D_SB), 0.02),
        'g_k': 1.0 + nrm(ks[15], (DEPTH, D_SB), 0.02),
        'b_sb': SB_BIAS_INIT + nrm(ks[25], (DEPTH, H_SB), 0.5),
        'g_sgu': 1.0 + nrm(ks[16], (DEPTH, W_SGU), 0.02),
        'w_sp': nrm(ks[17], (DEPTH, G_SGU, SGU_CHUNK, SGU_CHUNK), 0.5 * SGU_CHUNK ** -0.5),
        'b_sp': 1.0 + nrm(ks[18], (DEPTH, G_SGU, SGU_CHUNK), 0.02),
        'w_branch': nrm(ks[19], (DEPTH, N_BRANCH, BRANCH_W, D_MODEL), BRANCH_W ** -0.5),
        'w_out': nrm(ks[20], (DEPTH, D_MODEL, D_MODEL), D_MODEL ** -0.5),
        'w_pq': nrm(ks[21], (DEPTH, D_MODEL, PEER_HEADS * PEER_DQ), D_MODEL ** -0.5),
        'sub_keys': nrm(ks[22], (DEPTH, 2, PEER_NKEYS, PEER_DQ // 2), (PEER_DQ // 2) ** -0.5),
        'exp_u': nrm(ks[23], (DEPTH, PEER_EXPERTS, D_MODEL), D_MODEL ** -0.5),
        'exp_v': nrm(ks[24], (DEPTH, PEER_EXPERTS, D_MODEL), PEER_HEADS ** -0.5),
    }


def reference(x_prompt, x_sample, cache_k, cache_v, state_ret, page_table, c_prompt, c_sample,
              w_ada, b_ada, g_norm1, g_norm2, w_in, g_ret, g_q, g_k, b_sb, g_sgu, w_sp, b_sp,
              w_branch, w_out, w_pq, sub_keys, exp_u, exp_v):
    n_dec, n_pages = page_table.shape
    past_len = n_pages * cache_k.shape[2]
    pos_p = jnp.arange(x_prompt.shape[1])
    pos_s = past_len + jnp.arange(x_sample.shape[1])
    s0_p = jnp.zeros((x_prompt.shape[0], H_RET, DK_RET, DV_RET), jnp.float32)
    xp, xs = x_prompt, x_sample
    kp_l, vp_l, ks_l, vs_l, rp_l, rs_l, us_l = [], [], [], [], [], [], []
    for l in range(DEPTH):
        lw = (w_ada[l], b_ada[l], g_norm1[l], g_norm2[l], w_in[l], g_ret[l], g_q[l], g_k[l], b_sb[l], g_sgu[l],
              w_sp[l], b_sp[l], w_branch[l], w_out[l], w_pq[l], sub_keys[l], exp_u[l], exp_v[l])
        xp, rp, kp, vp, _ = trunk_layer(xp, c_prompt, pos_p, s0_p, None, None, 0, *lw)
        past_k = cache_k[page_table, l].reshape(n_dec, past_len, H_SB, D_SB)
        past_v = cache_v[page_table, l].reshape(n_dec, past_len, H_SB, D_SB)
        xs, rs, kn, vn, us = trunk_layer(xs, c_sample, pos_s, state_ret[:, l], past_k, past_v, past_len, *lw)
        kp_l.append(kp)
        vp_l.append(vp)
        ks_l.append(kn)
        vs_l.append(vn)
        rp_l.append(rp)
        rs_l.append(rs)
        us_l.append(us)
    return (xp, xs,
            jnp.stack(kp_l, axis=1), jnp.stack(vp_l, axis=1),
            jnp.stack(ks_l, axis=1), jnp.stack(vs_l, axis=1),
            jnp.stack(rp_l, axis=1), jnp.stack(rs_l, axis=1),
            jnp.stack(us_l, axis=1))
```

```python
import functools

import numpy as np
import jax
import jax.numpy as jnp
from jax import lax
from jax.experimental import pallas as pl
from jax.experimental.pallas import tpu as pltpu

F32 = jnp.float32
BF16 = jnp.bfloat16

D_MODEL = 1024
H_RET = 4
DK_RET = 128
RET_CHUNK = 128
ROPE_BASE = 10000.0
G_SGU = 4
SGU_CHUNK = 128
H_SB = 4
D_SB = 128
BRANCH_W = 512
PEER_HEADS = 8
PEER_NKEYS = 128
PEER_TOPK = 16
EPS = 1e-6

LANES = 128
MIB = 2 ** 20
NOT_RANKED = 99.0
N_GATE_COLS = 3 * D_MODEL
CB_RQ, CB_RK, CB_RV, CB_RG, CB_SU, CB_SV, CB_BQ, CB_BK, CB_BV = range(6, 15)


def _params(vmem_mib, n_grid):
    return pltpu.CompilerParams(dimension_semantics=("arbitrary",) * n_grid,
                                vmem_limit_bytes=vmem_mib * MIB)


def _ada_kernel(c_ref, w_ref, b_ref, o_ref):
    c = c_ref[...]
    s = (c * jax.nn.sigmoid(c)).astype(BF16)
    o_ref[0] = jnp.dot(s, w_ref[0].astype(BF16), preferred_element_type=F32) + b_ref[0]


def ada_mod(c_all, w_ada, b_ada):
    depth, d, n = w_ada.shape
    m = c_all.shape[0]
    tn = 1536
    return pl.pallas_call(
        _ada_kernel,
        grid=(depth, n // tn),
        in_specs=[pl.BlockSpec((m, d), lambda l, j: (0, 0)),
                  pl.BlockSpec((1, d, tn), lambda l, j: (l, 0, j)),
                  pl.BlockSpec((1, 1, tn), lambda l, j: (l, 0, j))],
        out_specs=pl.BlockSpec((1, m, tn), lambda l, j: (l, 0, j)),
        out_shape=jax.ShapeDtypeStruct((depth, m, n), F32),
        compiler_params=_params(40, 2),
        name="ada_mod",
    )(c_all, w_ada, b_ada.reshape(depth, 1, n))


def _nmm_kernel(x_ref, g_ref, sc_ref, sh_ref, w_ref, y_ref, *rest, emit_h, precise):
    hs = rest[-1]

    @pl.when(pl.program_id(1) == 0)
    def _():
        x = x_ref[...]
        ms = jnp.mean(x * x, axis=-1, keepdims=True)
        y = x * lax.rsqrt(ms + EPS) * g_ref[...]
        h = y * (1.0 + sc_ref[0]) + sh_ref[0]
        hs[...] = h.astype(hs.dtype)
        if emit_h:
            rest[0][...] = h.astype(BF16)

    if precise:
        y_ref[...] = jnp.dot(hs[...], w_ref[...], preferred_element_type=F32,
                             precision=lax.Precision.HIGHEST)
    else:
        y_ref[...] = jnp.dot(hs[...], w_ref[...], preferred_element_type=F32)


def norm_mod_matmul(x, g, sc, sh, w, *, tm, tn, rows_per_mod, emit_h=False, precise=False):
    t, d = x.shape
    n = w.shape[1]
    r = sc.shape[1]
    tiles_per_mod = rows_per_mod // tm
    mod_spec = pl.BlockSpec((1, r, d), lambda i, j: (i // tiles_per_mod, 0, 0))
    out_shape = [jax.ShapeDtypeStruct((t, n), F32)]
    out_specs = [pl.BlockSpec((tm, tn), lambda i, j: (i, j))]
    if emit_h:
        out_shape.append(jax.ShapeDtypeStruct((t, d), BF16))
        out_specs.append(pl.BlockSpec((tm, d), lambda i, j: (i, 0)))
    res = pl.pallas_call(
        functools.partial(_nmm_kernel, emit_h=emit_h, precise=precise),
        grid=(t // tm, n // tn),
        in_specs=[pl.BlockSpec((tm, d), lambda i, j: (i, 0)),
                  pl.BlockSpec((1, d), lambda i, j: (0, 0)),
                  mod_spec, mod_spec,
                  pl.BlockSpec((d, tn), lambda i, j: (0, j))],
        out_specs=out_specs,
        out_shape=out_shape,
        scratch_shapes=[pltpu.VMEM((tm, d), F32 if precise else BF16)],
        compiler_params=_params(48, 2),
        name="norm_mod_matmul",
    )(x, g.reshape(1, d), sc, sh, w)
    return res if emit_h else res[0]


def _rope(x, cos_full, sin_signed):
    return x * cos_full + pltpu.roll(x, 64, 1) * sin_signed


def _ret_gamma_logs():
    return np.log(1.0 - np.exp2(-5.0 - np.arange(H_RET, dtype=np.float64)))


def _dot_t(a, b):
    return lax.dot_general(a, b, (((0,), (0,)), ((), ())), preferred_element_type=F32)


def _dot_nt(a, b, **kw):
    return lax.dot_general(a, b, (((1,), (1,)), ((), ())), preferred_element_type=F32, **kw)


def _ret_head_qkv(q_ref, k_ref, v_ref, cos, sin, h):
    sl = slice(h * 128, (h + 1) * 128)
    q = _rope(q_ref[0, :, sl], cos, sin)
    k = _rope(k_ref[0, :, sl], cos, sin) * (DK_RET ** -0.5)
    v = v_ref[0, :, sl]
    return q, k, v


def _ret_finish(o, gate, g_row):
    mu = jnp.mean(o, axis=-1, keepdims=True)
    xc = o - mu
    var = jnp.mean(xc * xc, axis=-1, keepdims=True)
    y = xc * lax.rsqrt(var + EPS) * g_row
    return gate * jax.nn.sigmoid(gate) * y


def _ret_prompt_kernel(q_ref, k_ref, v_ref, g_ref, cos_ref, sin_ref, dm_ref, qd_ref, kd_ref, gr_ref,
                       o_ref, sfin_ref, state, *, chunk_dec, n_chunks):
    c = pl.program_id(1)

    @pl.when(c == 0)
    def _():
        state[...] = jnp.zeros_like(state)

    cos = cos_ref[...]
    sin = sin_ref[...]
    for h in range(H_RET):
        sl = slice(h * 128, (h + 1) * 128)
        q, k, v = _ret_head_qkv(q_ref, k_ref, v_ref, cos, sin, h)
        vb = v.astype(BF16)
        scores = _dot_nt(q.astype(BF16), k.astype(BF16)) * dm_ref[h]
        s_prev = state[h]
        o = jnp.dot(scores.astype(BF16), vb, preferred_element_type=F32)
        o = o + jnp.dot((q * qd_ref[:, sl]).astype(BF16), s_prev.astype(BF16), preferred_element_type=F32)
        state[h] = chunk_dec[h] * s_prev + _dot_t((k * kd_ref[:, sl]).astype(BF16), vb)
        o_ref[0, :, sl] = _ret_finish(o, g_ref[0, :, sl], gr_ref[:, sl]).astype(o_ref.dtype)

    @pl.when(c == n_chunks - 1)
    def _():
        sfin_ref[0] = state[...]


def _ret_tables(cs, rows):
    lg = _ret_gamma_logs()
    i = np.arange(rows) % cs
    seq = np.arange(rows) // cs
    diff = i[:, None] - i[None, :]
    same = seq[:, None] == seq[None, :]
    dmask = np.where((diff >= 0)[None] & same[None], np.exp(diff[None] * lg[:, None, None]), 0.0)
    k_dec = np.exp((cs - 1 - i)[:, None] * lg[None, :])
    q_dec = np.exp((i + 1)[:, None] * lg[None, :])
    k_dec = np.repeat(k_dec, 128, axis=1)
    q_dec = np.repeat(q_dec, 128, axis=1)
    chunk_dec = tuple(float(x) for x in np.exp(cs * lg))
    return (jnp.asarray(dmask, F32), jnp.asarray(q_dec, F32), jnp.asarray(k_dec, F32), chunk_dec)


def _rope_tables(pos):
    half = DK_RET // 2
    inv = ROPE_BASE ** (-jnp.arange(half, dtype=F32) / half)
    ang = pos.astype(F32)[:, None] * inv[None, :]
    cos = jnp.cos(ang)
    sin = jnp.sin(ang)
    return jnp.concatenate([cos, cos], axis=-1), jnp.concatenate([-sin, sin], axis=-1)


def retention_prompt(y3, rope_cos, rope_sin, g_ret):
    b, l, _ = y3.shape
    cs = RET_CHUNK
    nc = l // cs
    dmask, q_dec, k_dec, chunk_dec = _ret_tables(cs, cs)

    def col(cb):
        return pl.BlockSpec((1, cs, 512), lambda bi, c: (bi, c, cb))

    const2 = lambda shape: pl.BlockSpec(shape, lambda bi, c: (0,) * len(shape))
    return pl.pallas_call(
        functools.partial(_ret_prompt_kernel, chunk_dec=chunk_dec, n_chunks=nc),
        grid=(b, nc),
        in_specs=[col(CB_RQ), col(CB_RK), col(CB_RV), col(CB_RG),
                  pl.BlockSpec((cs, 128), lambda bi, c: (c, 0)),
                  pl.BlockSpec((cs, 128), lambda bi, c: (c, 0)),
                  const2((H_RET, cs, cs)), const2((cs, 512)), const2((cs, 512)), const2((1, 512))],
        out_specs=[pl.BlockSpec((1, cs, 512), lambda bi, c: (bi, c, 0)),
                   pl.BlockSpec((1, H_RET, 128, 128), lambda bi, c: (bi, 0, 0, 0))],
        out_shape=[jax.ShapeDtypeStruct((b, l, 512), BF16),
                   jax.ShapeDtypeStruct((b, H_RET, 128, 128), F32)],
        scratch_shapes=[pltpu.VMEM((H_RET, 128, 128), F32)],
        compiler_params=_params(32, 2),
        name="retention_prompt",
    )(y3, y3, y3, y3, rope_cos, rope_sin, dmask, q_dec, k_dec, g_ret.reshape(1, 512))


def _ret_sample_kernel(q_ref, k_ref, v_ref, g_ref, cos_ref, sin_ref, dm_ref, qd_ref, kd_ref, gr_ref, s0_ref,
                       o_ref, sfin_ref, o_state, *, chunk_dec, n_seq, cs):
    b = pl.program_id(0)

    @pl.when(b == 0)
    def _():
        o_state[...] = jnp.zeros_like(o_state)

    cos = cos_ref[...]
    sin = sin_ref[...]
    rows = q_ref.shape[1]
    in_seq = lax.shift_right_logical(lax.broadcasted_iota(jnp.int32, (rows, 128), 0), cs.bit_length() - 1) == b
    for h in range(H_RET):
        sl = slice(h * 128, (h + 1) * 128)
        q, k, v = _ret_head_qkv(q_ref, k_ref, v_ref, cos, sin, h)
        s_prev = s0_ref[0, 0, h]
        qm = jnp.where(in_seq, q * qd_ref[:, sl], 0.0)
        km = jnp.where(in_seq, k * kd_ref[:, sl], 0.0)
        o_state[:, sl] += jnp.dot(qm.astype(BF16), s_prev.astype(BF16), preferred_element_type=F32)
        sfin_ref[0, h] = chunk_dec[h] * s_prev + _dot_t(km.astype(BF16), v.astype(BF16))

    @pl.when(b == n_seq - 1)
    def _():
        for h in range(H_RET):
            sl = slice(h * 128, (h + 1) * 128)
            q, k, v = _ret_head_qkv(q_ref, k_ref, v_ref, cos, sin, h)
            scores = _dot_nt(q.astype(BF16), k.astype(BF16)) * dm_ref[h]
            o = jnp.dot(scores.astype(BF16), v.astype(BF16), preferred_element_type=F32) + o_state[:, sl]
            o_ref[0, :, sl] = _ret_finish(o, g_ref[0, :, sl], gr_ref[:, sl]).astype(o_ref.dtype)


def retention_sample(y3, rope_cos, rope_sin, g_ret, state_ret, layer, cs):
    _, t, _ = y3.shape
    n_seq = t // cs
    dmask, q_dec, k_dec, chunk_dec = _ret_tables(cs, t)

    def col(cb):
        return pl.BlockSpec((1, t, 512), lambda bi: (0, 0, cb))

    const = lambda shape: pl.BlockSpec(shape, lambda bi: (0,) * len(shape))
    return pl.pallas_call(
        functools.partial(_ret_sample_kernel, chunk_dec=chunk_dec, n_seq=n_seq, cs=cs),
        grid=(n_seq,),
        in_specs=[col(CB_RQ), col(CB_RK), col(CB_RV), col(CB_RG),
                  const((t, 128)), const((t, 128)),
                  const((H_RET, t, t)), const((t, 512)), const((t, 512)), const((1, 512)),
                  pl.BlockSpec((1, 1, H_RET, 128, 128), lambda bi: (bi, layer, 0, 0, 0))],
        out_specs=[pl.BlockSpec((1, t, 512), lambda bi: (0, 0, 0)),
                   pl.BlockSpec((1, H_RET, 128, 128), lambda bi: (bi, 0, 0, 0))],
        out_shape=[jax.ShapeDtypeStruct((1, t, 512), BF16),
                   jax.ShapeDtypeStruct((n_seq, H_RET, 128, 128), F32)],
        scratch_shapes=[pltpu.VMEM((t, 512), F32)],
        compiler_params=_params(32, 1),
        name="retention_sample",
    )(y3, y3, y3, y3, rope_cos, rope_sin, dmask, q_dec, k_dec, g_ret.reshape(1, 512), state_ret)


def _sgu_kernel(u_ref, v_ref, g_ref, w_ref, b_ref, o_ref, *rest, emit_vn):
    v = v_ref[0]
    ms = jnp.mean(v * v, axis=-1, keepdims=True)
    vn = v * lax.rsqrt(ms + EPS) * g_ref[...]
    if emit_vn:
        rest[0][0] = vn
    vb = vn.astype(BF16)
    for g in range(G_SGU):
        sl = slice(g * 128, (g + 1) * 128)
        mixed = jnp.dot(w_ref[g], vb[:, sl], preferred_element_type=F32) + b_ref[g]
        o_ref[0, :, sl] = (u_ref[0, :, sl] * mixed).astype(o_ref.dtype)


def spatial_gating(y3, g_sgu, w_mix, b_mix, emit_vn):
    b, l, _ = y3.shape
    rows = SGU_CHUNK
    nc = l // rows

    def col(cb):
        return pl.BlockSpec((1, rows, 512), lambda bi, c: (bi, c, cb))

    const = lambda shape: pl.BlockSpec(shape, lambda bi, c: (0,) * len(shape))
    out_spec = pl.BlockSpec((1, rows, 512), lambda bi, c: (bi, c, 0))
    out_shape = [jax.ShapeDtypeStruct((b, l, 512), BF16)]
    out_specs = [out_spec]
    if emit_vn:
        out_shape.append(jax.ShapeDtypeStruct((b, l, 512), F32))
        out_specs.append(out_spec)
    res = pl.pallas_call(
        functools.partial(_sgu_kernel, emit_vn=emit_vn),
        grid=(b, nc),
        in_specs=[col(CB_SU), col(CB_SV), const((1, 512)),
                  const((G_SGU, rows, rows)), const((G_SGU, rows, 128))],
        out_specs=out_specs, out_shape=out_shape,
        compiler_params=_params(32, 2),
        name="spatial_gating",
    )(y3, y3, g_sgu.reshape(1, 512), w_mix, b_mix)
    return res if emit_vn else (res[0], None)


def _qk_prep_kernel(q_ref, k_ref, v_ref, gq_ref, gk_ref, qn_ref, kn_ref, knb_ref, vb_ref):
    for h in range(H_SB):
        sl = slice(h * 128, (h + 1) * 128)
        q = q_ref[:, sl]
        k = k_ref[:, sl]
        qn = q * lax.rsqrt(jnp.mean(q * q, axis=-1, keepdims=True) + EPS) * gq_ref[...]
        kn = k * lax.rsqrt(jnp.mean(k * k, axis=-1, keepdims=True) + EPS) * gk_ref[...]
        qn_ref[:, sl] = qn.astype(BF16)
        kn_ref[:, sl] = kn
        knb_ref[:, sl] = kn.astype(BF16)
    vb_ref[...] = v_ref[...].astype(BF16)


def qk_prep(y, g_q, g_k, tm):
    t = y.shape[0]

    def col(cb):
        return pl.BlockSpec((tm, 512), lambda i: (i, cb))

    out_spec = pl.BlockSpec((tm, 512), lambda i: (i, 0))
    return pl.pallas_call(
        _qk_prep_kernel,
        grid=(t // tm,),
        in_specs=[col(CB_BQ), col(CB_BK), col(CB_BV),
                  pl.BlockSpec((1, 128), lambda i: (0, 0)), pl.BlockSpec((1, 128), lambda i: (0, 0))],
        out_specs=[out_spec] * 4,
        out_shape=[jax.ShapeDtypeStruct((t, 512), BF16), jax.ShapeDtypeStruct((t, 512), F32),
                   jax.ShapeDtypeStruct((t, 512), BF16), jax.ShapeDtypeStruct((t, 512), BF16)],
        compiler_params=_params(32, 1),
        name="qk_prep",
    )(y, y, y, g_q.reshape(1, 128), g_k.reshape(1, 128))


def _cumsum_weights():
    j = np.arange(128)
    later = (j[:, None] > j[None, :]).astype(np.float32)
    w = np.concatenate([later, np.ones((128, 128), np.float32)], axis=1)
    return jnp.asarray(np.concatenate([w, w], axis=0), BF16)


def _sb_tile(z, carry, uo, mask):
    lg = jnp.log(1.0 + jnp.exp(-jnp.abs(z)))
    log_1m = -(jnp.maximum(z, 0.0) + lg)
    if mask is not None:
        log_1m = jnp.where(mask, log_1m, 0.0)
    hi = log_1m.astype(BF16)
    lo = (log_1m - hi.astype(F32)).astype(BF16)
    r = jnp.dot(jnp.concatenate([hi, lo], axis=1), uo, preferred_element_type=F32)
    after = r[:, :128] + carry
    a = jnp.exp(jnp.minimum(z, 0.0) - lg + after)
    if mask is not None:
        a = jnp.where(mask, a, 0.0)
    return a, carry + r[:, 128:]


def _attn_prompt_kernel(bias_ref, q_ref, k_ref, v_ref, uo_ref, o_ref, *, tq):
    h = pl.program_id(1)
    i = pl.program_id(2)
    q = q_ref[0]
    bias = bias_ref[h]
    uo = uo_ref[...]
    scale = D_SB ** -0.5

    def tile(start, carry, acc, mask):
        k = k_ref[0, pl.ds(start, 128), :]
        v = v_ref[0, pl.ds(start, 128), :]
        z = _dot_nt(q, k) * scale + bias
        a, carry = _sb_tile(z, carry, uo, mask)
        return carry, acc + jnp.dot(a.astype(BF16), v, preferred_element_type=F32)

    row = lax.broadcasted_iota(jnp.int32, (tq, 128), 0)
    colk = lax.broadcasted_iota(jnp.int32, (tq, 128), 1)
    zeros = jnp.zeros((tq, 128), F32)
    carry, acc = tile(pl.multiple_of(i * tq, 128), zeros, zeros, colk < row)

    def body(jj, c):
        return tile(pl.multiple_of((i - jj) * 128, 128), c[0], c[1], None)

    carry, acc = lax.fori_loop(1, i + 1, body, (carry, acc))
    o_ref[0] = acc.astype(o_ref.dtype)


def attention_prompt(qn, knb, vb, b_sb, uo):
    b, l, _ = qn.shape
    tq = 128
    return pl.pallas_call(
        functools.partial(_attn_prompt_kernel, tq=tq),
        grid=(b, H_SB, l // tq),
        in_specs=[pl.BlockSpec(memory_space=pltpu.SMEM),
                  pl.BlockSpec((1, tq, 128), lambda bi, h, i: (bi, i, h)),
                  pl.BlockSpec((1, l, 128), lambda bi, h, i: (bi, 0, h)),
                  pl.BlockSpec((1, l, 128), lambda bi, h, i: (bi, 0, h)),
                  pl.BlockSpec((256, 256), lambda bi, h, i: (0, 0))],
        out_specs=pl.BlockSpec((1, tq, 128), lambda bi, h, i: (bi, i, h)),
        out_shape=jax.ShapeDtypeStruct((b, l, 512), BF16),
        compiler_params=_params(32, 3),
        name="attention_prompt",
    )(b_sb, qn, knb, vb, uo)


ROWS_PER_HEAD = 8


def _attn_sample_kernel(pt_ref, qbd_ref, kn_ref, vn_ref, brow_ref, uo_ref, *rest, n_slots, n_steps):
    kp = rest[:n_slots]
    vp = rest[n_slots:2 * n_slots]
    o_ref, carry_s, acc_s = rest[2 * n_slots:]
    j = pl.program_id(1)
    qbd = qbd_ref[0]
    uo = uo_ref[...]
    brow = brow_ref[...]
    scale = D_SB ** -0.5
    m = qbd.shape[0]

    def tile(k, v, mask):
        z = _dot_nt(qbd, k) * scale + brow
        a, carry = _sb_tile(z, carry_s[...], uo, mask)
        carry_s[...] = carry
        acc_s[...] += jnp.dot(a.astype(BF16), v, preferred_element_type=F32)

    @pl.when(j == 0)
    def _():
        carry_s[...] = jnp.zeros_like(carry_s)
        acc_s[...] = jnp.zeros_like(acc_s)
        tok = lax.broadcasted_iota(jnp.int32, (m, 128), 0) & (ROWS_PER_HEAD - 1)
        key = lax.broadcasted_iota(jnp.int32, (m, 128), 1)
        tile(kn_ref[0], vn_ref[0], key < tok)

    for s in range(n_slots):
        tile(kp[s][0, 0].astype(BF16), vp[s][0, 0].astype(BF16), None)

    @pl.when(j == n_steps - 1)
    def _():
        acc = acc_s[...]
        row_h = lax.shift_right_logical(lax.broadcasted_iota(jnp.int32, acc.shape, 0), 3)
        col_h = lax.shift_right_logical(lax.broadcasted_iota(jnp.int32, acc.shape, 1), 7)
        own = jnp.where(row_h == col_h, acc, 0.0)
        o = own[0:ROWS_PER_HEAD]
        for h in range(1, H_SB):
            o = o + own[h * ROWS_PER_HEAD:(h + 1) * ROWS_PER_HEAD]
        o_ref[0] = o


def attention_sample(qbd, k_new, v_new, brow, uo, cache_k, cache_v, page_table, layer, n_slots=4):
    n_seq, n_pages = page_table.shape
    n_steps = n_pages // n_slots
    m = qbd.shape[1]

    def page_spec(s):
        return pl.BlockSpec((1, 1, 128, 512),
                            lambda bi, j, pt: (pt[bi, n_pages - 1 - (j * n_slots + s)], layer, 0, 0))

    per_seq = lambda shape: pl.BlockSpec(shape, lambda bi, j, pt: (bi,) + (0,) * (len(shape) - 1))
    const = lambda shape: pl.BlockSpec(shape, lambda bi, j, pt: (0,) * len(shape))
    grid_spec = pltpu.PrefetchScalarGridSpec(
        num_scalar_prefetch=1,
        grid=(n_seq, n_steps),
        in_specs=[per_seq((1, m, 512)), per_seq((1, 128, 512)), per_seq((1, 128, 512)),
                  const((m, 128)), const((256, 256))]
                 + [page_spec(s) for s in range(n_slots)] * 2,
        out_specs=per_seq((1, ROWS_PER_HEAD, 512)),
        scratch_shapes=[pltpu.VMEM((m, 128), F32), pltpu.VMEM((m, 512), F32)],
    )
    return pl.pallas_call(
        functools.partial(_attn_sample_kernel, n_slots=n_slots, n_steps=n_steps),
        grid_spec=grid_spec,
        out_shape=jax.ShapeDtypeStruct((n_seq, ROWS_PER_HEAD, 512), F32),
        compiler_params=_params(32, 2),
        name="attention_sample",
    )(page_table, qbd, k_new, v_new, brow, uo, *([cache_k] * n_slots), *([cache_v] * n_slots))


def _merge_kernel(x_ref, oa_ref, ob_ref, oc_ref, gates_ref, wb_ref, wo_ref, ga_ref, o_ref):
    merged = None
    for br, o_b in enumerate((oa_ref, ob_ref, oc_ref)):
        gate = jax.nn.sigmoid(gates_ref[:, br * D_MODEL:(br + 1) * D_MODEL])
        term = gate * jnp.dot(o_b[...], wb_ref[br], preferred_element_type=F32)
        merged = term if merged is None else merged + term
    y = jnp.dot(merged.astype(BF16), wo_ref[...], preferred_element_type=F32)
    o_ref[...] = x_ref[...] + ga_ref[0] * y


def merge_out(x, o_ret, o_sgu, o_sb, y, w_branch, w_out, ga, *, tm, rows_per_mod):
    t, d = x.shape
    r = ga.shape[1]
    tiles_per_mod = rows_per_mod // tm
    row = lambda w: pl.BlockSpec((tm, w), lambda i: (i, 0))
    return pl.pallas_call(
        _merge_kernel,
        grid=(t // tm,),
        in_specs=[row(d), row(512), row(512), row(512), row(N_GATE_COLS),
                  pl.BlockSpec((3, BRANCH_W, d), lambda i: (0, 0, 0)),
                  pl.BlockSpec((d, d), lambda i: (0, 0)),
                  pl.BlockSpec((1, r, d), lambda i: (i // tiles_per_mod, 0, 0))],
        out_specs=row(d),
        out_shape=jax.ShapeDtypeStruct((t, d), F32),
        compiler_params=_params(48, 1),
        name="merge_out",
    )(x, o_ret, o_sgu, o_sb, y, w_branch, w_out, ga)


def _topk_rows(vals, ids, k):
    rank = jnp.full(vals.shape, NOT_RANKED, F32)
    tops = []
    for a in range(k):
        m = jnp.max(vals, axis=0, keepdims=True)
        first = jnp.min(jnp.where(vals == m, ids, 1e9), axis=0, keepdims=True)
        hit = ids == first
        rank = jnp.where(hit, float(a), rank)
        vals = jnp.where(hit, -jnp.inf, vals)
        tops.append(m)
    return tops, rank


def _cand_rows(a):
    need = PEER_TOPK // (a + 1)
    return -(-need // 8) * 8


def _peer_select_kernel(q_ref, sk_ref, c1_ref, f1_ref, r2_ref, f2_ref):
    tt = q_ref.shape[0]
    key_id = lax.broadcasted_iota(jnp.int32, (PEER_NKEYS, tt), 0).astype(F32)
    s_t, tops, ranks = [], [], []
    for p in range(2):
        s = _dot_nt(sk_ref[p], q_ref[:, p * 128:(p + 1) * 128], precision=lax.Precision.HIGHEST)
        t_p, r_p = _topk_rows(s, key_id, PEER_TOPK)
        s_t.append(s)
        tops.append(t_p)
        ranks.append(r_p)
    top_id = lax.broadcasted_iota(jnp.int32, (PEER_TOPK, tt), 0)
    top2 = jnp.zeros((PEER_TOPK, tt), F32)
    for b in range(PEER_TOPK):
        top2 = jnp.where(top_id == b, tops[1][b], top2)
    slabs, ids = [], []
    for a in range(PEER_TOPK):
        nb = _cand_rows(a)
        slabs.append(tops[0][a] + top2[0:nb])
        ids.append(lax.broadcasted_iota(jnp.int32, (nb, tt), 0).astype(F32) + float(a * PEER_TOPK))
    cand = jnp.concatenate(slabs, axis=0)
    cand_id = jnp.concatenate(ids, axis=0)
    _, crank = _topk_rows(cand, cand_id, PEER_TOPK)
    chosen = jnp.where(crank < NOT_RANKED, 1.0, 0.0)
    e2_top = jnp.exp(top2 - tops[1][0])
    z = jnp.zeros((1, tt), F32)
    counts = []
    off = 0
    for a in range(PEER_TOPK):
        nb = _cand_rows(a)
        ch = chosen[off:off + nb]
        counts.append(jnp.sum(ch, axis=0, keepdims=True))
        z = z + jnp.exp(tops[0][a] - tops[0][0]) * jnp.sum(ch * e2_top[0:nb], axis=0, keepdims=True)
        off += nb
    c1 = jnp.zeros((PEER_NKEYS, tt), F32)
    for a in range(PEER_TOPK):
        c1 = jnp.where(ranks[0] == float(a), counts[a], c1)
    c1_ref[0] = c1
    f1_ref[0] = jnp.exp(s_t[0] - tops[0][0]) / z
    r2_ref[0] = ranks[1]
    f2_ref[0] = jnp.exp(s_t[1] - tops[1][0])


def peer_select(q, sub_keys, tt):
    t = q.shape[0]
    shape = jax.ShapeDtypeStruct((PEER_HEADS, PEER_NKEYS, t), F32)
    spec = pl.BlockSpec((1, PEER_NKEYS, tt), lambda i, h: (h, 0, i))
    return pl.pallas_call(
        _peer_select_kernel,
        grid=(t // tt, PEER_HEADS),
        in_specs=[pl.BlockSpec((tt, 256), lambda i, h: (i, h)),
                  pl.BlockSpec((2, PEER_NKEYS, 128), lambda i, h: (0, 0, 0))],
        out_specs=[spec] * 4, out_shape=[shape] * 4,
        compiler_params=_params(32, 2),
        name="peer_select",
    )(q, sub_keys)


def _gelu_tanh(x):
    return 0.5 * x * (1.0 + jnp.tanh(0.7978845608028654 * (x + 0.044715 * (x * x * x))))


def _peer_dense_kernel(h_ref, u_ref, vt_ref, c1_ref, f1_ref, r2_ref, f2_ref, x_ref, ga_ref, o_ref,
                       acc, act, wact, *, n_rows, n_steps):
    e = pl.program_id(1)
    tt = h_ref.shape[0]

    @pl.when(e == 0)
    def _():
        acc[...] = jnp.zeros_like(acc)

    act[...] = _gelu_tanh(_dot_nt(u_ref[...], h_ref[...]))
    for il in range(n_rows):
        rows = slice(il * PEER_NKEYS, (il + 1) * PEER_NKEYS)
        for tc in range(tt // LANES):
            lanes = slice(tc * LANES, (tc + 1) * LANES)
            w = jnp.zeros((PEER_NKEYS, LANES), F32)
            for h in range(PEER_HEADS):
                c1 = c1_ref[h, il:il + 1, lanes]
                f1 = f1_ref[h, il:il + 1, lanes]
                w = w + jnp.where(r2_ref[h, :, lanes] < c1, f2_ref[h, :, lanes], 0.0) * f1
            wact[rows, lanes] = (w * act[rows, lanes]).astype(BF16)
    acc[...] += jnp.dot(vt_ref[...], wact[...], preferred_element_type=F32)

    @pl.when(e == n_steps - 1)
    def _():
        o_ref[...] = x_ref[...] + ga_ref[0] * acc[...].T


def peer_dense(x, h2, exp_u, exp_vt, sel, ga, *, tt, rows_per_mod, n_rows=8):
    t, d = x.shape
    n_exp = exp_u.shape[0]
    te = n_rows * PEER_NKEYS
    n_steps = n_exp // te
    r = ga.shape[1]
    tiles_per_mod = rows_per_mod // tt
    row_spec = pl.BlockSpec((PEER_HEADS, n_rows, tt), lambda i, e: (0, e, i))
    full_spec = pl.BlockSpec((PEER_HEADS, PEER_NKEYS, tt), lambda i, e: (0, 0, i))
    return pl.pallas_call(
        functools.partial(_peer_dense_kernel, n_rows=n_rows, n_steps=n_steps),
        grid=(t // tt, n_steps),
        in_specs=[pl.BlockSpec((tt, d), lambda i, e: (i, 0)),
                  pl.BlockSpec((te, d), lambda i, e: (e, 0)),
                  pl.BlockSpec((d, te), lambda i, e: (0, e)),
                  row_spec, row_spec, full_spec, full_spec,
                  pl.BlockSpec((tt, d), lambda i, e: (i, 0)),
                  pl.BlockSpec((1, r, d), lambda i, e: (i // tiles_per_mod, 0, 0))],
        out_specs=pl.BlockSpec((tt, d), lambda i, e: (i, 0)),
        out_shape=jax.ShapeDtypeStruct((t, d), F32),
        scratch_shapes=[pltpu.VMEM((d, tt), F32), pltpu.VMEM((te, tt), F32), pltpu.VMEM((te, tt), BF16)],
        compiler_params=_params(56, 2),
        name="peer_dense",
    )(h2, exp_u, exp_vt, *sel, x, ga)


def _token_tile(rows_per_mod):
    return 512 if rows_per_mod % 512 == 0 else rows_per_mod


def _layer(x, mods, lw, *, n_batch, sample):
    t, d = x.shape
    l = t // n_batch
    sh1, sc1, ga1, sh2, sc2, ga2 = mods
    rows_per_mod = l if sample is None else t
    tm = _token_tile(rows_per_mod)
    y = norm_mod_matmul(x, lw["g_norm1"], sc1, sh1, lw["w_in"], tm=tm, tn=1536, rows_per_mod=rows_per_mod)

    qn, kn, knb, vb = qk_prep(y, lw["g_q"], lw["g_k"], tm)
    if sample is None:
        y3 = y.reshape(n_batch, l, -1)
        o_ret, s_new = retention_prompt(y3, lw["rope_cos"], lw["rope_sin"], lw["g_ret"])
        o_sgu, vn_rows = spatial_gating(y3, lw["g_sgu"], lw["w_mix"], lw["b_mix"], emit_vn=False)
        o_sb = attention_prompt(qn.reshape(n_batch, l, 512), knb.reshape(n_batch, l, 512),
                                vb.reshape(n_batch, l, 512), lw["b_sb"], lw["uo"])
    else:
        y3 = y.reshape(1, t, -1)
        o_ret, s_new = retention_sample(y3, lw["rope_cos"], lw["rope_sin"], lw["g_ret"],
                                        sample["state_ret"], sample["layer"], l)
        o_sgu, vn_rows = spatial_gating(y3, lw["g_sgu"], lw["w_mix"], lw["b_mix"], emit_vn=True)
        q4 = qn.reshape(n_batch, l, H_SB, 1, D_SB)
        eye = jnp.eye(H_SB, dtype=BF16).reshape(1, 1, H_SB, H_SB, 1)
        qbd = jnp.transpose(q4 * eye, (0, 2, 1, 3, 4))
        qbd = jnp.pad(qbd, ((0, 0), (0, 0), (0, ROWS_PER_HEAD - l), (0, 0), (0, 0)))
        qbd = qbd.reshape(n_batch, H_SB * ROWS_PER_HEAD, 512)
        pad_keys = ((0, 0), (0, 128 - l), (0, 0))
        k_new = jnp.pad(knb.reshape(n_batch, l, 512), pad_keys)
        v_new = jnp.pad(vb.reshape(n_batch, l, 512), pad_keys)
        brow = jnp.broadcast_to(jnp.repeat(lw["b_sb"], ROWS_PER_HEAD)[:, None], (H_SB * ROWS_PER_HEAD, 128))
        o8 = attention_sample(qbd, k_new, v_new, brow, lw["uo"], sample["cache_k"], sample["cache_v"],
                              sample["page_table"], sample["layer"])
        o_sb = o8[:, :l].reshape(t, 512).astype(BF16)

    x = merge_out(x, o_ret.reshape(t, 512), o_sgu.reshape(t, 512), o_sb.reshape(t, 512), y,
                  lw["w_branch"], lw["w_out"], ga1, tm=tm, rows_per_mod=rows_per_mod)

    pq, h2 = norm_mod_matmul(x, lw["g_norm2"], sc2, sh2, lw["w_pq"], tm=tm, tn=512,
                             rows_per_mod=rows_per_mod, emit_h=True, precise=True)
    tt = tm
    sel = peer_select(pq, lw["sub_keys"], tt)
    x = peer_dense(x, h2, lw["exp_u"], lw["exp_vt"], sel, ga2, tt=tt, rows_per_mod=rows_per_mod)
    v_raw = y[:, CB_BV * 512:(CB_BV + 1) * 512]
    return x, s_new, kn, v_raw, vn_rows


def kernel(x_prompt, x_sample, cache_k, cache_v, state_ret, page_table, c_prompt, c_sample, w_ada, b_ada,
           g_norm1, g_norm2, w_in, g_ret, g_q, g_k, b_sb, g_sgu, w_sp, b_sp, w_branch, w_out, w_pq,
           sub_keys, exp_u, exp_v):
    n_b, seq, d = x_prompt.shape
    n_s, dec, _ = x_sample.shape
    depth = w_in.shape[0]
    n_pool, _, page, _, _ = cache_k.shape
    past_len = page_table.shape[1] * page
    t_s = n_s * dec

    mod = ada_mod(jnp.concatenate([c_prompt, c_sample], axis=0), w_ada, b_ada)
    cache_k4 = cache_k.reshape(n_pool, depth, page, H_SB * D_SB)
    cache_v4 = cache_v.reshape(n_pool, depth, page, H_SB * D_SB)
    uo = _cumsum_weights()
    cos_p, sin_p = _rope_tables(jnp.arange(seq))
    cos_s, sin_s = _rope_tables(past_len + (jnp.arange(t_s) % dec))

    pos = np.arange(t_s)
    same_seq = jnp.asarray((pos[:, None] // dec) == (pos[None, :] // dec))
    tri = jnp.asarray(np.tril(np.ones((SGU_CHUNK, SGU_CHUNK), bool)))

    xp = x_prompt.reshape(n_b * seq, d)
    xs = x_sample.reshape(t_s, d)
    outs = {k: [] for k in ("kp", "vp", "ks", "vs", "rp", "rs", "us")}
    for l in range(depth):
        w_in_l = jnp.concatenate([w_in[l][:, -N_GATE_COLS:], w_in[l][:, :-N_GATE_COLS]], axis=1).astype(BF16)
        shared = dict(g_norm1=g_norm1[l], g_norm2=g_norm2[l], w_in=w_in_l, g_ret=g_ret[l], g_q=g_q[l], g_k=g_k[l],
                      b_sb=b_sb[l], g_sgu=g_sgu[l], w_branch=w_branch[l].astype(BF16), w_out=w_out[l].astype(BF16),
                      w_pq=w_pq[l], sub_keys=sub_keys[l], exp_u=exp_u[l].astype(BF16),
                      exp_vt=exp_v[l].T.astype(BF16), uo=uo)
        mods_p = [m.reshape(n_b, 1, d) for m in jnp.split(mod[l, :n_b], 6, axis=-1)]
        mods_s = [jnp.repeat(m, dec, axis=0).reshape(1, t_s, d) for m in jnp.split(mod[l, n_b:], 6, axis=-1)]

        w_tri = jnp.where(tri[None], w_sp[l], 0.0)
        lw_p = dict(shared, rope_cos=cos_p, rope_sin=sin_p, w_mix=w_tri.astype(BF16),
                    b_mix=jnp.broadcast_to(b_sp[l][:, :, None], (G_SGU, SGU_CHUNK, 128)))
        xp, rp, kp, vp, _ = _layer(xp, mods_p, lw_p, n_batch=n_b, sample=None)

        w_blk = jnp.where(same_seq[None], jnp.tile(w_tri[:, :dec, :dec], (1, n_s, n_s)), 0.0)
        b_blk = jnp.tile(b_sp[l][:, :dec], (1, n_s))
        lw_s = dict(shared, rope_cos=cos_s, rope_sin=sin_s, w_mix=w_blk.astype(BF16),
                    b_mix=jnp.broadcast_to(b_blk[:, :, None], (G_SGU, t_s, 128)))
        sample = dict(state_ret=state_ret, cache_k=cache_k4, cache_v=cache_v4, page_table=page_table, layer=l)
        xs, rs, ks, vs, us = _layer(xs, mods_s, lw_s, n_batch=n_s, sample=sample)

        outs["kp"].append(kp.reshape(n_b, seq, H_SB, D_SB))
        outs["vp"].append(vp.reshape(n_b, seq, H_SB, D_SB))
        outs["ks"].append(ks.reshape(n_s, dec, H_SB, D_SB))
        outs["vs"].append(vs.reshape(n_s, dec, H_SB, D_SB))
        outs["rp"].append(rp)
        outs["rs"].append(rs)
        outs["us"].append(us.reshape(n_s, dec, -1))
    stack = lambda name: jnp.stack(outs[name], axis=1)
    return (xp.reshape(n_b, seq, d), xs.reshape(n_s, dec, d),
            stack("kp"), stack("vp"), stack("ks"), stack("vs"), stack("rp"), stack("rs"), stack("us"))
```

```python
import functools

import numpy as np
import jax
import jax.numpy as jnp
from jax import lax
from jax.experimental import pallas as pl
from jax.experimental.pallas import tpu as pltpu

F32 = jnp.float32
BF16 = jnp.bfloat16

D_MODEL = 1024
H_RET = 4
DK_RET = 128
RET_CHUNK = 128
ROPE_BASE = 10000.0
G_SGU = 4
SGU_CHUNK = 128
H_SB = 4
D_SB = 128
BRANCH_W = 512
PEER_HEADS = 8
PEER_NKEYS = 128
PEER_TOPK = 16
EPS = 1e-6

LOG2E = 1.4426950408889634
Q_LOGIT_SCALE = D_SB ** -0.5 * LOG2E
LANES = 128
BF16_ROWS = 16
MIB = 2 ** 20
NOT_RANKED = 99.0
N_GATE_COLS = 3 * D_MODEL
CB_RQ, CB_RK, CB_RV, CB_RG, CB_SU, CB_SV, CB_BQ, CB_BK, CB_BV = range(6, 15)


def _params(vmem_mib, n_grid):
    return pltpu.CompilerParams(dimension_semantics=("arbitrary",) * n_grid,
                                vmem_limit_bytes=vmem_mib * MIB)


def _ada_kernel(c_ref, w_ref, b_ref, o_ref):
    c = c_ref[...]
    s = (c * jax.nn.sigmoid(c)).astype(BF16)
    o_ref[0] = jnp.dot(s, w_ref[0].astype(BF16), preferred_element_type=F32) + b_ref[0]


def ada_mod(c_all, w_ada, b_ada):
    depth, d, n = w_ada.shape
    m = c_all.shape[0]
    tn = 1536
    return pl.pallas_call(
        _ada_kernel,
        grid=(depth, n // tn),
        in_specs=[pl.BlockSpec((m, d), lambda l, j: (0, 0)),
                  pl.BlockSpec((1, d, tn), lambda l, j: (l, 0, j)),
                  pl.BlockSpec((1, 1, tn), lambda l, j: (l, 0, j))],
        out_specs=pl.BlockSpec((1, m, tn), lambda l, j: (l, 0, j)),
        out_shape=jax.ShapeDtypeStruct((depth, m, n), F32),
        compiler_params=_params(40, 2),
        name="ada_mod",
    )(c_all, w_ada, b_ada.reshape(depth, 1, n))


def _nmm_kernel(x_ref, g_ref, sc_ref, sh_ref, w_ref, y_ref, *rest, emit_h, precise):
    hs = rest[-1]

    @pl.when(pl.program_id(1) == 0)
    def _():
        x = x_ref[...]
        ms = jnp.mean(x * x, axis=-1, keepdims=True)
        y = x * lax.rsqrt(ms + EPS) * g_ref[...]
        h = y * (1.0 + sc_ref[0]) + sh_ref[0]
        hs[...] = h.astype(hs.dtype)
        if emit_h:
            rest[0][...] = h.astype(BF16)

    if precise:
        y_ref[...] = jnp.dot(hs[...], w_ref[...], preferred_element_type=F32,
                             precision=lax.Precision.HIGHEST)
    else:
        y_ref[...] = jnp.dot(hs[...], w_ref[...], preferred_element_type=F32)


def norm_mod_matmul(x, g, sc, sh, w, *, tm, tn, rows_per_mod, emit_h=False, precise=False):
    t, d = x.shape
    n = w.shape[1]
    r = sc.shape[1]
    tiles_per_mod = rows_per_mod // tm
    mod_spec = pl.BlockSpec((1, r, d), lambda i, j: (i // tiles_per_mod, 0, 0))
    out_shape = [jax.ShapeDtypeStruct((t, n), F32)]
    out_specs = [pl.BlockSpec((tm, tn), lambda i, j: (i, j))]
    if emit_h:
        out_shape.append(jax.ShapeDtypeStruct((t, d), BF16))
        out_specs.append(pl.BlockSpec((tm, d), lambda i, j: (i, 0)))
    res = pl.pallas_call(
        functools.partial(_nmm_kernel, emit_h=emit_h, precise=precise),
        grid=(t // tm, n // tn),
        in_specs=[pl.BlockSpec((tm, d), lambda i, j: (i, 0)),
                  pl.BlockSpec((1, d), lambda i, j: (0, 0)),
                  mod_spec, mod_spec,
                  pl.BlockSpec((d, tn), lambda i, j: (0, j))],
        out_specs=out_specs,
        out_shape=out_shape,
        scratch_shapes=[pltpu.VMEM((tm, d), F32 if precise else BF16)],
        compiler_params=_params(48, 2),
        name="norm_mod_matmul",
    )(x, g.reshape(1, d), sc, sh, w)
    return res if emit_h else res[0]


def _rope(x, cos_full, sin_signed):
    return x * cos_full + pltpu.roll(x, 64, 1) * sin_signed


def _ret_gamma_logs():
    return np.log(1.0 - np.exp2(-5.0 - np.arange(H_RET, dtype=np.float64)))


def _dot_t(a, b):
    return lax.dot_general(a, b, (((0,), (0,)), ((), ())), preferred_element_type=F32)


def _dot_nt(a, b, **kw):
    return lax.dot_general(a, b, (((1,), (1,)), ((), ())), preferred_element_type=F32, **kw)


def _ret_head_qkv(q_ref, k_ref, v_ref, cos, sin, h):
    sl = slice(h * 128, (h + 1) * 128)
    q = _rope(q_ref[0, :, sl], cos, sin)
    k = _rope(k_ref[0, :, sl], cos, sin) * (DK_RET ** -0.5)
    v = v_ref[0, :, sl]
    return q, k, v


def _ret_finish(o, gate, g_row):
    mu = jnp.mean(o, axis=-1, keepdims=True)
    xc = o - mu
    var = jnp.mean(xc * xc, axis=-1, keepdims=True)
    y = xc * lax.rsqrt(var + EPS) * g_row
    return gate * jax.nn.sigmoid(gate) * y


def _ret_prompt_kernel(q_ref, k_ref, v_ref, g_ref, cos_ref, sin_ref, dm_ref, qd_ref, kd_ref, gr_ref,
                       o_ref, sfin_ref, state, *, chunk_dec, n_chunks):
    c = pl.program_id(1)

    @pl.when(c == 0)
    def _():
        state[...] = jnp.zeros_like(state)

    cos = cos_ref[...]
    sin = sin_ref[...]
    for h in range(H_RET):
        sl = slice(h * 128, (h + 1) * 128)
        q, k, v = _ret_head_qkv(q_ref, k_ref, v_ref, cos, sin, h)
        vb = v.astype(BF16)
        scores = _dot_nt(q.astype(BF16), k.astype(BF16)) * dm_ref[h]
        s_prev = state[h]
        o = jnp.dot(scores.astype(BF16), vb, preferred_element_type=F32)
        o = o + jnp.dot((q * qd_ref[:, sl]).astype(BF16), s_prev.astype(BF16), preferred_element_type=F32)
        state[h] = chunk_dec[h] * s_prev + _dot_t((k * kd_ref[:, sl]).astype(BF16), vb)
        o_ref[0, :, sl] = _ret_finish(o, g_ref[0, :, sl], gr_ref[:, sl]).astype(o_ref.dtype)

    @pl.when(c == n_chunks - 1)
    def _():
        sfin_ref[0] = state[...]


def _ret_tables(cs, rows):
    lg = _ret_gamma_logs()
    i = np.arange(rows) % cs
    seq = np.arange(rows) // cs
    diff = i[:, None] - i[None, :]
    same = seq[:, None] == seq[None, :]
    dmask = np.where((diff >= 0)[None] & same[None], np.exp(diff[None] * lg[:, None, None]), 0.0)
    k_dec = np.exp((cs - 1 - i)[:, None] * lg[None, :])
    q_dec = np.exp((i + 1)[:, None] * lg[None, :])
    k_dec = np.repeat(k_dec, 128, axis=1)
    q_dec = np.repeat(q_dec, 128, axis=1)
    chunk_dec = tuple(float(x) for x in np.exp(cs * lg))
    return (jnp.asarray(dmask, F32), jnp.asarray(q_dec, F32), jnp.asarray(k_dec, F32), chunk_dec)


def _rope_tables(pos):
    half = DK_RET // 2
    inv = ROPE_BASE ** (-jnp.arange(half, dtype=F32) / half)
    ang = pos.astype(F32)[:, None] * inv[None, :]
    cos = jnp.cos(ang)
    sin = jnp.sin(ang)
    return jnp.concatenate([cos, cos], axis=-1), jnp.concatenate([-sin, sin], axis=-1)


def retention_prompt(y3, rope_cos, rope_sin, g_ret):
    b, l, _ = y3.shape
    cs = RET_CHUNK
    nc = l // cs
    dmask, q_dec, k_dec, chunk_dec = _ret_tables(cs, cs)

    def col(cb):
        return pl.BlockSpec((1, cs, 512), lambda bi, c: (bi, c, cb))

    const2 = lambda shape: pl.BlockSpec(shape, lambda bi, c: (0,) * len(shape))
    return pl.pallas_call(
        functools.partial(_ret_prompt_kernel, chunk_dec=chunk_dec, n_chunks=nc),
        grid=(b, nc),
        in_specs=[col(CB_RQ), col(CB_RK), col(CB_RV), col(CB_RG),
                  pl.BlockSpec((cs, 128), lambda bi, c: (c, 0)),
                  pl.BlockSpec((cs, 128), lambda bi, c: (c, 0)),
                  const2((H_RET, cs, cs)), const2((cs, 512)), const2((cs, 512)), const2((1, 512))],
        out_specs=[pl.BlockSpec((1, cs, 512), lambda bi, c: (bi, c, 0)),
                   pl.BlockSpec((1, H_RET, 128, 128), lambda bi, c: (bi, 0, 0, 0))],
        out_shape=[jax.ShapeDtypeStruct((b, l, 512), BF16),
                   jax.ShapeDtypeStruct((b, H_RET, 128, 128), F32)],
        scratch_shapes=[pltpu.VMEM((H_RET, 128, 128), F32)],
        compiler_params=_params(32, 2),
        name="retention_prompt",
    )(y3, y3, y3, y3, rope_cos, rope_sin, dmask, q_dec, k_dec, g_ret.reshape(1, 512))


def _ret_sample_kernel(q_ref, k_ref, v_ref, g_ref, cos_ref, sin_ref, dm_ref, qd_ref, kd_ref, gr_ref, s0_ref,
                       o_ref, sfin_ref, o_state, *, chunk_dec, n_seq, cs):
    b = pl.program_id(0)

    @pl.when(b == 0)
    def _():
        o_state[...] = jnp.zeros_like(o_state)

    cos = cos_ref[...]
    sin = sin_ref[...]
    rows = q_ref.shape[1]
    in_seq = lax.shift_right_logical(lax.broadcasted_iota(jnp.int32, (rows, 128), 0), cs.bit_length() - 1) == b
    for h in range(H_RET):
        sl = slice(h * 128, (h + 1) * 128)
        q, k, v = _ret_head_qkv(q_ref, k_ref, v_ref, cos, sin, h)
        s_prev = s0_ref[0, 0, h]
        qm = jnp.where(in_seq, q * qd_ref[:, sl], 0.0)
        km = jnp.where(in_seq, k * kd_ref[:, sl], 0.0)
        o_state[:, sl] += jnp.dot(qm.astype(BF16), s_prev.astype(BF16), preferred_element_type=F32)
        sfin_ref[0, h] = chunk_dec[h] * s_prev + _dot_t(km.astype(BF16), v.astype(BF16))

    @pl.when(b == n_seq - 1)
    def _():
        for h in range(H_RET):
            sl = slice(h * 128, (h + 1) * 128)
            q, k, v = _ret_head_qkv(q_ref, k_ref, v_ref, cos, sin, h)
            scores = _dot_nt(q.astype(BF16), k.astype(BF16)) * dm_ref[h]
            o = jnp.dot(scores.astype(BF16), v.astype(BF16), preferred_element_type=F32) + o_state[:, sl]
            o_ref[0, :, sl] = _ret_finish(o, g_ref[0, :, sl], gr_ref[:, sl]).astype(o_ref.dtype)


def retention_sample(y3, rope_cos, rope_sin, g_ret, state_ret, layer, cs):
    _, t, _ = y3.shape
    n_seq = t // cs
    dmask, q_dec, k_dec, chunk_dec = _ret_tables(cs, t)

    def col(cb):
        return pl.BlockSpec((1, t, 512), lambda bi: (0, 0, cb))

    const = lambda shape: pl.BlockSpec(shape, lambda bi: (0,) * len(shape))
    return pl.pallas_call(
        functools.partial(_ret_sample_kernel, chunk_dec=chunk_dec, n_seq=n_seq, cs=cs),
        grid=(n_seq,),
        in_specs=[col(CB_RQ), col(CB_RK), col(CB_RV), col(CB_RG),
                  const((t, 128)), const((t, 128)),
                  const((H_RET, t, t)), const((t, 512)), const((t, 512)), const((1, 512)),
                  pl.BlockSpec((1, 1, H_RET, 128, 128), lambda bi: (bi, layer, 0, 0, 0))],
        out_specs=[pl.BlockSpec((1, t, 512), lambda bi: (0, 0, 0)),
                   pl.BlockSpec((1, H_RET, 128, 128), lambda bi: (bi, 0, 0, 0))],
        out_shape=[jax.ShapeDtypeStruct((1, t, 512), BF16),
                   jax.ShapeDtypeStruct((n_seq, H_RET, 128, 128), F32)],
        scratch_shapes=[pltpu.VMEM((t, 512), F32)],
        compiler_params=_params(32, 1),
        name="retention_sample",
    )(y3, y3, y3, y3, rope_cos, rope_sin, dmask, q_dec, k_dec, g_ret.reshape(1, 512), state_ret)


def _sgu_kernel(u_ref, v_ref, g_ref, w_ref, b_ref, o_ref, *rest, emit_vn):
    v = v_ref[0]
    ms = jnp.mean(v * v, axis=-1, keepdims=True)
    vn = v * lax.rsqrt(ms + EPS) * g_ref[...]
    if emit_vn:
        rest[0][0] = vn
    vb = vn.astype(BF16)
    for g in range(G_SGU):
        sl = slice(g * 128, (g + 1) * 128)
        mixed = jnp.dot(w_ref[g], vb[:, sl], preferred_element_type=F32) + b_ref[g]
        o_ref[0, :, sl] = (u_ref[0, :, sl] * mixed).astype(o_ref.dtype)


def spatial_gating(y3, g_sgu, w_mix, b_mix, emit_vn):
    b, l, _ = y3.shape
    rows = SGU_CHUNK
    nc = l // rows

    def col(cb):
        return pl.BlockSpec((1, rows, 512), lambda bi, c: (bi, c, cb))

    const = lambda shape: pl.BlockSpec(shape, lambda bi, c: (0,) * len(shape))
    out_spec = pl.BlockSpec((1, rows, 512), lambda bi, c: (bi, c, 0))
    out_shape = [jax.ShapeDtypeStruct((b, l, 512), BF16)]
    out_specs = [out_spec]
    if emit_vn:
        out_shape.append(jax.ShapeDtypeStruct((b, l, 512), F32))
        out_specs.append(out_spec)
    res = pl.pallas_call(
        functools.partial(_sgu_kernel, emit_vn=emit_vn),
        grid=(b, nc),
        in_specs=[col(CB_SU), col(CB_SV), const((1, 512)),
                  const((G_SGU, rows, rows)), const((G_SGU, rows, 128))],
        out_specs=out_specs, out_shape=out_shape,
        compiler_params=_params(32, 2),
        name="spatial_gating",
    )(y3, y3, g_sgu.reshape(1, 512), w_mix, b_mix)
    return res if emit_vn else (res[0], None)


def _qk_prep_kernel(q_ref, k_ref, v_ref, gq_ref, gk_ref, qn_ref, kn_ref, knb_ref, vb_ref):
    for h in range(H_SB):
        sl = slice(h * 128, (h + 1) * 128)
        q = q_ref[:, sl]
        k = k_ref[:, sl]
        qn = q * lax.rsqrt(jnp.mean(q * q, axis=-1, keepdims=True) + EPS) * gq_ref[...]
        kn = k * lax.rsqrt(jnp.mean(k * k, axis=-1, keepdims=True) + EPS) * gk_ref[...]
        qn_ref[:, sl] = (qn * Q_LOGIT_SCALE).astype(BF16)
        kn_ref[:, sl] = kn
        knb_ref[:, sl] = kn.astype(BF16)
    vb_ref[...] = v_ref[...].astype(BF16)


def qk_prep(y, g_q, g_k, tm):
    t = y.shape[0]

    def col(cb):
        return pl.BlockSpec((tm, 512), lambda i: (i, cb))

    out_spec = pl.BlockSpec((tm, 512), lambda i: (i, 0))
    return pl.pallas_call(
        _qk_prep_kernel,
        grid=(t // tm,),
        in_specs=[col(CB_BQ), col(CB_BK), col(CB_BV),
                  pl.BlockSpec((1, 128), lambda i: (0, 0)), pl.BlockSpec((1, 128), lambda i: (0, 0))],
        out_specs=[out_spec] * 4,
        out_shape=[jax.ShapeDtypeStruct((t, 512), BF16), jax.ShapeDtypeStruct((t, 512), F32),
                   jax.ShapeDtypeStruct((t, 512), BF16), jax.ShapeDtypeStruct((t, 512), BF16)],
        compiler_params=_params(32, 1),
        name="qk_prep",
    )(y, y, y, g_q.reshape(1, 128), g_k.reshape(1, 128))


def _cumsum_weights():
    j = np.arange(128)
    later = (j[:, None] > j[None, :]).astype(np.float32)
    w = np.concatenate([later, np.ones((128, 128), np.float32)], axis=1)
    return jnp.asarray(np.concatenate([w, w], axis=0), BF16)


def _sb_tile(z2, carry, uo, mask):
    base, r = _sb_stage1(z2, uo, mask)
    return _sb_stage2(base, r, carry)


MASKED_LOG = -1e30


def _sb_stage1(z2, uo, mask):
    lg2 = jnp.log(1.0 + jnp.exp2(-jnp.abs(z2))) * LOG2E
    log_1m = -(jnp.maximum(z2, 0.0) + lg2)
    base = jnp.minimum(z2, 0.0) - lg2
    if mask is not None:
        log_1m = jnp.where(mask, log_1m, 0.0)
        base = jnp.where(mask, base, MASKED_LOG)
    hi = log_1m.astype(BF16)
    lo = (log_1m - hi.astype(F32)).astype(BF16)
    r = jnp.dot(jnp.concatenate([hi, lo], axis=1), uo, preferred_element_type=F32)
    return base, r


def _sb_stage2(base, r, carry):
    a = jnp.exp2(base + (r[:, :128] + carry))
    return a, carry + r[:, 128:]


def _attn_prompt_kernel(bias_ref, q_ref, k_ref, v_ref, uo_ref, o_ref, *, tq):
    h = pl.program_id(1)
    i = pl.program_id(2)
    q = q_ref[0]
    bias2 = bias_ref[h] * LOG2E
    uo = uo_ref[...]
    n_band = tq // 128
    row = lax.broadcasted_iota(jnp.int32, (tq, 128), 0)
    colk = lax.broadcasted_iota(jnp.int32, (tq, 128), 1)

    def tile_start(t):
        return pl.multiple_of(jnp.maximum((i + 1) * n_band - 1 - t, 0) * 128, 128)

    def logits(t):
        return _dot_nt(q, k_ref[0, pl.ds(tile_start(t), 128), :]) + bias2

    def finish(t_prev, acc, log_a_prev):
        v = v_ref[0, pl.ds(tile_start(t_prev), 128), :]
        return acc + jnp.dot(jnp.exp2(log_a_prev).astype(BF16), v, preferred_element_type=F32)

    def trip(t, c, mask):
        carry, acc, z2, log_a_prev = c
        z_next = logits(t + 1)
        acc = finish(jnp.maximum(t - 1, 0), acc, log_a_prev)
        base, r = _sb_stage1(z2, uo, mask)
        return carry + r[:, 128:], acc, z_next, base + (r[:, :128] + carry)

    zero = jnp.zeros((tq, 128), F32)
    c = (zero, zero, logits(0), jnp.full((tq, 128), MASKED_LOG, F32))
    for t in range(n_band):
        c = trip(t, c, colk + (n_band - 1 - t) * 128 < row)
    c = lax.fori_loop(n_band, (i + 1) * n_band, lambda t, c: trip(t, c, None), c)
    acc = finish((i + 1) * n_band - 1, c[1], c[3])
    o_ref[0] = acc.astype(o_ref.dtype)


def attention_prompt(qn, knb, vb, b_sb, uo, tq=512):
    b, l, _ = qn.shape
    tq = min(tq, l)
    return pl.pallas_call(
        functools.partial(_attn_prompt_kernel, tq=tq),
        grid=(b, H_SB, l // tq),
        in_specs=[pl.BlockSpec(memory_space=pltpu.SMEM),
                  pl.BlockSpec((1, tq, 128), lambda bi, h, i: (bi, i, h)),
                  pl.BlockSpec((1, l, 128), lambda bi, h, i: (bi, 0, h)),
                  pl.BlockSpec((1, l, 128), lambda bi, h, i: (bi, 0, h)),
                  pl.BlockSpec((256, 256), lambda bi, h, i: (0, 0))],
        out_specs=pl.BlockSpec((1, tq, 128), lambda bi, h, i: (bi, i, h)),
        out_shape=jax.ShapeDtypeStruct((b, l, 512), BF16),
        compiler_params=_params(32, 3),
        name="attention_prompt",
    )(b_sb, qn, knb, vb, uo)


ROWS_PER_HEAD = 8


def _attn_sample_kernel(pt_ref, q_ref, kn_ref, vn_ref, brow_ref, uo_ref, *rest, n_slots, n_steps):
    kp = rest[:n_slots]
    vp = rest[n_slots:2 * n_slots]
    o_ref, carry_s, acc_s = rest[2 * n_slots:]
    j = pl.program_id(1)
    uo = uo_ref[...]
    brow = brow_ref[...]
    heads = range(H_SB)
    rows = [slice(h * ROWS_PER_HEAD, (h + 1) * ROWS_PER_HEAD) for h in heads]
    qs = [q_ref[0, rows[h], :] for h in heads]

    def logits(k_of_head):
        return jnp.concatenate([_dot_nt(qs[h], k_of_head(h)) for h in heads], axis=0) + brow

    def accumulate(a, v_of_head):
        a = a.astype(BF16)
        for h in heads:
            acc_s[rows[h], :] += jnp.dot(a[rows[h]], v_of_head(h), preferred_element_type=F32)

    @pl.when(j == 0)
    def _():
        acc_s[...] = jnp.zeros_like(acc_s)
        m = brow.shape[0]
        tok = lax.broadcasted_iota(jnp.int32, (m, 128), 0) & (ROWS_PER_HEAD - 1)
        key = lax.broadcasted_iota(jnp.int32, (m, 128), 1)
        a, carry = _sb_tile(logits(lambda h: kn_ref[0, h]), jnp.zeros((m, 128), F32), uo, key < tok)
        carry_s[...] = carry
        accumulate(a, lambda h: vn_ref[0, h])

    def page_head(ref, h):
        return ref[0, 0, pl.ds(h, 128, stride=H_SB), :].astype(BF16)

    stage1 = [_sb_stage1(logits(lambda h, s=s: page_head(kp[s], h)), uo, None) for s in range(n_slots)]
    carry = carry_s[...]
    for s in range(n_slots):
        a, carry = _sb_stage2(*stage1[s], carry)
        accumulate(a, lambda h, s=s: page_head(vp[s], h))
    carry_s[...] = carry

    @pl.when(j == n_steps - 1)
    def _():
        o_ref[0] = acc_s[...]


def attention_sample(q8, k_new, v_new, brow, uo, cache_k, cache_v, page_table, layer, n_slots=4):
    n_seq, n_pages = page_table.shape
    n_steps = n_pages // n_slots
    m = q8.shape[1]
    page_rows = cache_k.shape[2]

    def page_spec(s):
        return pl.BlockSpec((1, 1, page_rows, 128),
                            lambda bi, j, pt: (pt[bi, n_pages - 1 - (j * n_slots + s)], layer, 0, 0))

    per_seq = lambda shape: pl.BlockSpec(shape, lambda bi, j, pt: (bi,) + (0,) * (len(shape) - 1))
    const = lambda shape: pl.BlockSpec(shape, lambda bi, j, pt: (0,) * len(shape))
    grid_spec = pltpu.PrefetchScalarGridSpec(
        num_scalar_prefetch=1,
        grid=(n_seq, n_steps),
        in_specs=[per_seq((1, m, 128)), per_seq((1, H_SB, 128, 128)), per_seq((1, H_SB, 128, 128)),
                  const((m, 128)), const((256, 256))]
                 + [page_spec(s) for s in range(n_slots)] * 2,
        out_specs=per_seq((1, m, 128)),
        scratch_shapes=[pltpu.VMEM((m, 128), F32), pltpu.VMEM((m, 128), F32)],
    )
    return pl.pallas_call(
        functools.partial(_attn_sample_kernel, n_slots=n_slots, n_steps=n_steps),
        grid_spec=grid_spec,
        out_shape=jax.ShapeDtypeStruct((n_seq, m, 128), F32),
        compiler_params=_params(32, 2),
        name="attention_sample",
    )(page_table, q8, k_new, v_new, brow, uo, *([cache_k] * n_slots), *([cache_v] * n_slots))


def _merge_kernel(x_ref, oa_ref, ob_ref, oc_ref, gates_ref, wb_ref, wo_ref, ga_ref, o_ref):
    merged = None
    for br, o_b in enumerate((oa_ref, ob_ref, oc_ref)):
        gate = jax.nn.sigmoid(gates_ref[:, br * D_MODEL:(br + 1) * D_MODEL])
        term = gate * jnp.dot(o_b[...], wb_ref[br], preferred_element_type=F32)
        merged = term if merged is None else merged + term
    y = jnp.dot(merged.astype(BF16), wo_ref[...], preferred_element_type=F32)
    o_ref[...] = x_ref[...] + ga_ref[0] * y


def merge_out(x, o_ret, o_sgu, o_sb, y, w_branch, w_out, ga, *, tm, rows_per_mod):
    t, d = x.shape
    r = ga.shape[1]
    tiles_per_mod = rows_per_mod // tm
    row = lambda w: pl.BlockSpec((tm, w), lambda i: (i, 0))
    return pl.pallas_call(
        _merge_kernel,
        grid=(t // tm,),
        in_specs=[row(d), row(512), row(512), row(512), row(N_GATE_COLS),
                  pl.BlockSpec((3, BRANCH_W, d), lambda i: (0, 0, 0)),
                  pl.BlockSpec((d, d), lambda i: (0, 0)),
                  pl.BlockSpec((1, r, d), lambda i: (i // tiles_per_mod, 0, 0))],
        out_specs=row(d),
        out_shape=jax.ShapeDtypeStruct((t, d), F32),
        compiler_params=_params(48, 1),
        name="merge_out",
    )(x, o_ret, o_sgu, o_sb, y, w_branch, w_out, ga)


def _topk_rows(vals, ids, k):
    rank = jnp.full(vals.shape, NOT_RANKED, F32)
    tops = []
    for a in range(k):
        m = jnp.max(vals, axis=0, keepdims=True)
        first = jnp.min(jnp.where(vals == m, ids, 1e9), axis=0, keepdims=True)
        hit = ids == first
        rank = jnp.where(hit, float(a), rank)
        vals = jnp.where(hit, -jnp.inf, vals)
        tops.append(m)
    return tops, rank, vals


def _cand_rows(a):
    need = PEER_TOPK // (a + 1)
    return -(-need // 8) * 8


def _peer_select_chunk(s1, s2):
    tt = s1.shape[1]
    key_id = lax.broadcasted_iota(jnp.int32, (PEER_NKEYS, tt), 0).astype(F32)
    top1, rank1, _ = _topk_rows(s1, key_id, PEER_TOPK)
    top2_rows, rank2, _ = _topk_rows(s2, key_id, PEER_TOPK)
    top_id = lax.broadcasted_iota(jnp.int32, (PEER_TOPK, tt), 0)
    top2 = jnp.zeros((PEER_TOPK, tt), F32)
    for b in range(PEER_TOPK):
        top2 = jnp.where(top_id == b, top2_rows[b], top2)
    slabs, ids = [], []
    for a in range(PEER_TOPK):
        nb = _cand_rows(a)
        slabs.append(top1[a] + top2[0:nb])
        ids.append(lax.broadcasted_iota(jnp.int32, (nb, tt), 0).astype(F32) + float(a * PEER_TOPK))
    _, _, left = _topk_rows(jnp.concatenate(slabs, axis=0), jnp.concatenate(ids, axis=0), PEER_TOPK)
    chosen = jnp.where(left == -jnp.inf, 1.0, 0.0)
    e2_top = jnp.exp(top2 - top2_rows[0])
    z = jnp.zeros((1, tt), F32)
    counts = []
    off = 0
    for a in range(PEER_TOPK):
        nb = _cand_rows(a)
        ch = chosen[off:off + nb]
        counts.append(jnp.sum(ch, axis=0, keepdims=True))
        z = z + jnp.exp(top1[a] - top1[0]) * jnp.sum(ch * e2_top[0:nb], axis=0, keepdims=True)
        off += nb
    c1 = jnp.zeros((PEER_NKEYS, tt), F32)
    for a in range(PEER_TOPK):
        c1 = jnp.where(rank1 == float(a), counts[a], c1)
    f1 = jnp.exp(s1 - top1[0]) / z
    return c1, f1, rank2.astype(BF16), jnp.exp(s2 - top2_rows[0]).astype(BF16)


def _peer_select_kernel(q_ref, sk_ref, c1_ref, f1_ref, r2_ref, f2_ref):
    tt = q_ref.shape[0]
    grouped = (PEER_NKEYS // BF16_ROWS, BF16_ROWS, LANES)
    for c in range(tt // LANES):
        tok = slice(c * LANES, (c + 1) * LANES)
        s1, s2 = [_dot_nt(sk_ref[p], q_ref[tok, p * 128:(p + 1) * 128], precision=lax.Precision.HIGHEST)
                  for p in range(2)]
        c1, f1, r2, f2 = _peer_select_chunk(s1, s2)
        c1_ref[0, :, tok] = c1
        f1_ref[0, :, tok] = f1
        r2_ref[0, :, :, tok] = r2.reshape(grouped)
        f2_ref[0, :, :, tok] = f2.reshape(grouped)


def peer_select(q, sub_keys, tt):
    t = q.shape[0]
    n_groups = PEER_NKEYS // BF16_ROWS
    shape1 = jax.ShapeDtypeStruct((PEER_HEADS, PEER_NKEYS, t), F32)
    shape2 = jax.ShapeDtypeStruct((PEER_HEADS, n_groups, BF16_ROWS, t), BF16)
    spec1 = pl.BlockSpec((1, PEER_NKEYS, tt), lambda i, h: (h, 0, i))
    spec2 = pl.BlockSpec((1, n_groups, BF16_ROWS, tt), lambda i, h: (h, 0, 0, i))
    return pl.pallas_call(
        _peer_select_kernel,
        grid=(t // tt, PEER_HEADS),
        in_specs=[pl.BlockSpec((tt, 256), lambda i, h: (i, h)),
                  pl.BlockSpec((2, PEER_NKEYS, 128), lambda i, h: (0, 0, 0))],
        out_specs=[spec1, spec1, spec2, spec2], out_shape=[shape1, shape1, shape2, shape2],
        compiler_params=_params(32, 2),
        name="peer_select",
    )(q, sub_keys)


def _gelu_tanh(x):
    return 0.5 * x * (1.0 + jnp.tanh(0.7978845608028654 * (x + 0.044715 * (x * x * x))))


def _peer_dense_kernel(h_ref, u_ref, vt_ref, c1_ref, f1_ref, r2_ref, f2_ref, x_ref, ga_ref, o_ref,
                       acc, act, wact, *, n_rows, n_steps):
    e = pl.program_id(1)
    tt = h_ref.shape[0]

    @pl.when(e == 0)
    def _():
        acc[...] = jnp.zeros_like(acc)

    chunk = min(tt, 2 * LANES)
    zero = jnp.zeros((BF16_ROWS, chunk), BF16)
    n_groups = PEER_NKEYS // BF16_ROWS
    te = u_ref.shape[0]
    act[...] = _gelu_tanh(_dot_nt(u_ref[...], h_ref[...])).astype(BF16).reshape(act.shape)

    for il in range(n_rows):
        for tc in range(tt // chunk):
            lanes = slice(tc * chunk, (tc + 1) * chunk)
            w = [zero] * n_groups
            for h in range(PEER_HEADS):
                c1 = jnp.broadcast_to(c1_ref[h, il:il + 1, lanes], (BF16_ROWS, chunk)).astype(BF16)
                f1 = jnp.broadcast_to(f1_ref[h, il:il + 1, lanes], (BF16_ROWS, chunk)).astype(BF16)
                for g in range(n_groups):
                    w[g] = w[g] + jnp.where(r2_ref[h, g, :, lanes] < c1, f2_ref[h, g, :, lanes], zero) * f1
            for g in range(n_groups):
                wact[il * n_groups + g, :, lanes] = w[g] * act[il * n_groups + g, :, lanes]
    acc[...] += jnp.dot(vt_ref[...], wact[...].reshape(te, tt), preferred_element_type=F32)

    @pl.when(e == n_steps - 1)
    def _():
        o_ref[...] = x_ref[...] + ga_ref[0] * acc[...].T


def peer_dense(x, h2, exp_u, exp_vt, sel, ga, *, tt, rows_per_mod, n_rows=8):
    t, d = x.shape
    n_exp = exp_u.shape[0]
    te = n_rows * PEER_NKEYS
    n_steps = n_exp // te
    r = ga.shape[1]
    tiles_per_mod = rows_per_mod // tt
    row_spec = pl.BlockSpec((PEER_HEADS, n_rows, tt), lambda i, e: (0, e, i))
    n_groups = PEER_NKEYS // BF16_ROWS
    full_spec = pl.BlockSpec((PEER_HEADS, n_groups, BF16_ROWS, tt), lambda i, e: (0, 0, 0, i))
    grouped = pltpu.VMEM((te // BF16_ROWS, BF16_ROWS, tt), BF16)
    return pl.pallas_call(
        functools.partial(_peer_dense_kernel, n_rows=n_rows, n_steps=n_steps),
        grid=(t // tt, n_steps),
        in_specs=[pl.BlockSpec((tt, d), lambda i, e: (i, 0)),
                  pl.BlockSpec((te, d), lambda i, e: (e, 0)),
                  pl.BlockSpec((d, te), lambda i, e: (0, e)),
                  row_spec, row_spec, full_spec, full_spec,
                  pl.BlockSpec((tt, d), lambda i, e: (i, 0)),
                  pl.BlockSpec((1, r, d), lambda i, e: (i // tiles_per_mod, 0, 0))],
        out_specs=pl.BlockSpec((tt, d), lambda i, e: (i, 0)),
        out_shape=jax.ShapeDtypeStruct((t, d), F32),
        scratch_shapes=[pltpu.VMEM((d, tt), F32), grouped, grouped],
        compiler_params=_params(56, 2),
        name="peer_dense",
    )(h2, exp_u, exp_vt, *sel, x, ga)


def _token_tile(rows_per_mod):
    return 512 if rows_per_mod % 512 == 0 else rows_per_mod


def _layer(x, mods, lw, *, n_batch, sample):
    t, d = x.shape
    l = t // n_batch
    sh1, sc1, ga1, sh2, sc2, ga2 = mods
    rows_per_mod = l if sample is None else t
    tm = _token_tile(rows_per_mod)
    y = norm_mod_matmul(x, lw["g_norm1"], sc1, sh1, lw["w_in"], tm=tm, tn=1536, rows_per_mod=rows_per_mod)

    qn, kn, knb, vb = qk_prep(y, lw["g_q"], lw["g_k"], tm)
    if sample is None:
        y3 = y.reshape(n_batch, l, -1)
        o_ret, s_new = retention_prompt(y3, lw["rope_cos"], lw["rope_sin"], lw["g_ret"])
        o_sgu, vn_rows = spatial_gating(y3, lw["g_sgu"], lw["w_mix"], lw["b_mix"], emit_vn=False)
        o_sb = attention_prompt(qn.reshape(n_batch, l, 512), knb.reshape(n_batch, l, 512),
                                vb.reshape(n_batch, l, 512), lw["b_sb"], lw["uo"])
    else:
        y3 = y.reshape(1, t, -1)
        o_ret, s_new = retention_sample(y3, lw["rope_cos"], lw["rope_sin"], lw["g_ret"],
                                        sample["state_ret"], sample["layer"], l)
        o_sgu, vn_rows = spatial_gating(y3, lw["g_sgu"], lw["w_mix"], lw["b_mix"], emit_vn=True)
        def head_major(a, rows):
            a = jnp.transpose(a.reshape(n_batch, l, H_SB, D_SB), (0, 2, 1, 3))
            return jnp.pad(a, ((0, 0), (0, 0), (0, rows - l), (0, 0)))

        q8 = head_major(qn, ROWS_PER_HEAD).reshape(n_batch, H_SB * ROWS_PER_HEAD, D_SB)
        brow = jnp.broadcast_to(jnp.repeat(lw["b_sb"] * LOG2E, ROWS_PER_HEAD)[:, None],
                                (H_SB * ROWS_PER_HEAD, 128))
        o8 = attention_sample(q8, head_major(knb, 128), head_major(vb, 128), brow, lw["uo"],
                              sample["cache_k"], sample["cache_v"], sample["page_table"], sample["layer"])
        o8 = o8.reshape(n_batch, H_SB, ROWS_PER_HEAD, D_SB)[:, :, :l]
        o_sb = jnp.transpose(o8, (0, 2, 1, 3)).reshape(t, 512).astype(BF16)

    x = merge_out(x, o_ret.reshape(t, 512), o_sgu.reshape(t, 512), o_sb.reshape(t, 512), y,
                  lw["w_branch"], lw["w_out"], ga1, tm=tm, rows_per_mod=rows_per_mod)

    pq, h2 = norm_mod_matmul(x, lw["g_norm2"], sc2, sh2, lw["w_pq"], tm=tm, tn=512,
                             rows_per_mod=rows_per_mod, emit_h=True, precise=True)
    tt = tm
    sel = peer_select(pq, lw["sub_keys"], tt)
    x = peer_dense(x, h2, lw["exp_u"], lw["exp_vt"], sel, ga2, tt=tt, rows_per_mod=rows_per_mod)
    v_raw = y[:, CB_BV * 512:(CB_BV + 1) * 512]
    return x, s_new, kn, v_raw, vn_rows


def kernel(x_prompt, x_sample, cache_k, cache_v, state_ret, page_table, c_prompt, c_sample, w_ada, b_ada,
           g_norm1, g_norm2, w_in, g_ret, g_q, g_k, b_sb, g_sgu, w_sp, b_sp, w_branch, w_out, w_pq,
           sub_keys, exp_u, exp_v):
    n_b, seq, d = x_prompt.shape
    n_s, dec, _ = x_sample.shape
    depth = w_in.shape[0]
    n_pool, _, page, _, _ = cache_k.shape
    past_len = page_table.shape[1] * page
    t_s = n_s * dec

    mod = ada_mod(jnp.concatenate([c_prompt, c_sample], axis=0), w_ada, b_ada)
    cache_k4 = cache_k.reshape(n_pool, depth, page * H_SB, D_SB)
    cache_v4 = cache_v.reshape(n_pool, depth, page * H_SB, D_SB)
    uo = _cumsum_weights()
    cos_p, sin_p = _rope_tables(jnp.arange(seq))
    cos_s, sin_s = _rope_tables(past_len + (jnp.arange(t_s) % dec))

    pos = np.arange(t_s)
    same_seq = jnp.asarray((pos[:, None] // dec) == (pos[None, :] // dec))
    tri = jnp.asarray(np.tril(np.ones((SGU_CHUNK, SGU_CHUNK), bool)))

    xp = x_prompt.reshape(n_b * seq, d)
    xs = x_sample.reshape(t_s, d)
    outs = {k: [] for k in ("kp", "vp", "ks", "vs", "rp", "rs", "us")}
    for l in range(depth):
        w_in_l = jnp.concatenate([w_in[l][:, -N_GATE_COLS:], w_in[l][:, :-N_GATE_COLS]], axis=1).astype(BF16)
        shared = dict(g_norm1=g_norm1[l], g_norm2=g_norm2[l], w_in=w_in_l, g_ret=g_ret[l], g_q=g_q[l], g_k=g_k[l],
                      b_sb=b_sb[l], g_sgu=g_sgu[l], w_branch=w_branch[l].astype(BF16), w_out=w_out[l].astype(BF16),
                      w_pq=w_pq[l], sub_keys=sub_keys[l], exp_u=exp_u[l].astype(BF16),
                      exp_vt=exp_v[l].T.astype(BF16), uo=uo)
        mods_p = [m.reshape(n_b, 1, d) for m in jnp.split(mod[l, :n_b], 6, axis=-1)]
        mods_s = [jnp.repeat(m, dec, axis=0).reshape(1, t_s, d) for m in jnp.split(mod[l, n_b:], 6, axis=-1)]

        w_tri = jnp.where(tri[None], w_sp[l], 0.0)
        lw_p = dict(shared, rope_cos=cos_p, rope_sin=sin_p, w_mix=w_tri.astype(BF16),
                    b_mix=jnp.broadcast_to(b_sp[l][:, :, None], (G_SGU, SGU_CHUNK, 128)))
        xp, rp, kp, vp, _ = _layer(xp, mods_p, lw_p, n_batch=n_b, sample=None)

        w_blk = jnp.where(same_seq[None], jnp.tile(w_tri[:, :dec, :dec], (1, n_s, n_s)), 0.0)
        b_blk = jnp.tile(b_sp[l][:, :dec], (1, n_s))
        lw_s = dict(shared, rope_cos=cos_s, rope_sin=sin_s, w_mix=w_blk.astype(BF16),
                    b_mix=jnp.broadcast_to(b_blk[:, :, None], (G_SGU, t_s, 128)))
        sample = dict(state_ret=state_ret, cache_k=cache_k4, cache_v=cache_v4, page_table=page_table, layer=l)
        xs, rs, ks, vs, us = _layer(xs, mods_s, lw_s, n_batch=n_s, sample=sample)

        outs["kp"].append(kp.reshape(n_b, seq, H_SB, D_SB))
        outs["vp"].append(vp.reshape(n_b, seq, H_SB, D_SB))
        outs["ks"].append(ks.reshape(n_s, dec, H_SB, D_SB))
        outs["vs"].append(vs.reshape(n_s, dec, H_SB, D_SB))
        outs["rp"].append(rp)
        outs["rs"].append(rs)
        outs["us"].append(us.reshape(n_s, dec, -1))
    stack = lambda name: jnp.stack(outs[name], axis=1)
    return (xp.reshape(n_b, seq, d), xs.reshape(n_s, dec, d),
            stack("kp"), stack("vp"), stack("ks"), stack("vs"), stack("rp"), stack("rs"), stack("us"))
```

```python
import functools

import numpy as np
import jax
import jax.numpy as jnp
from jax import lax
from jax.experimental import pallas as pl
from jax.experimental.pallas import tpu as pltpu

F32 = jnp.float32
BF16 = jnp.bfloat16

D_MODEL = 1024
H_RET = 4
DK_RET = 128
RET_CHUNK = 128
ROPE_BASE = 10000.0
G_SGU = 4
SGU_CHUNK = 128
H_SB = 4
D_SB = 128
BRANCH_W = 512
PEER_HEADS = 8
PEER_NKEYS = 128
PEER_TOPK = 16
EPS = 1e-6

LOG2E = 1.4426950408889634
Q_LOGIT_SCALE = D_SB ** -0.5 * LOG2E
LANES = 128
BF16_ROWS = 16
MIB = 2 ** 20
NOT_RANKED = 99.0
N_GATE_COLS = 3 * D_MODEL
CB_RQ, CB_RK, CB_RV, CB_RG, CB_SU, CB_SV, CB_BQ, CB_BK, CB_BV = range(6, 15)


def _params(vmem_mib, n_grid):
    return pltpu.CompilerParams(dimension_semantics=("arbitrary",) * n_grid,
                                vmem_limit_bytes=vmem_mib * MIB)


def _ada_kernel(c_ref, w_ref, b_ref, o_ref):
    c = c_ref[...]
    s = (c * jax.nn.sigmoid(c)).astype(BF16)
    o_ref[0] = jnp.dot(s, w_ref[0].astype(BF16), preferred_element_type=F32) + b_ref[0]


def ada_mod(c_all, w_ada, b_ada):
    depth, d, n = w_ada.shape
    m = c_all.shape[0]
    tn = 1536
    return pl.pallas_call(
        _ada_kernel,
        grid=(depth, n // tn),
        in_specs=[pl.BlockSpec((m, d), lambda l, j: (0, 0)),
                  pl.BlockSpec((1, d, tn), lambda l, j: (l, 0, j)),
                  pl.BlockSpec((1, 1, tn), lambda l, j: (l, 0, j))],
        out_specs=pl.BlockSpec((1, m, tn), lambda l, j: (l, 0, j)),
        out_shape=jax.ShapeDtypeStruct((depth, m, n), F32),
        compiler_params=_params(40, 2),
        name="ada_mod",
    )(c_all, w_ada, b_ada.reshape(depth, 1, n))


def _nmm_kernel(x_ref, g_ref, sc_ref, sh_ref, *rest, emit_h, precise):
    n_w = 2 if precise else 1
    w_refs, y_ref, scratch = rest[:n_w], rest[n_w], rest[-n_w:]

    @pl.when(pl.program_id(1) == 0)
    def _():
        x = x_ref[...]
        ms = jnp.mean(x * x, axis=-1, keepdims=True)
        y = x * lax.rsqrt(ms + EPS) * g_ref[...]
        h = y * (1.0 + sc_ref[0]) + sh_ref[0]
        h_hi = h.astype(BF16)
        scratch[0][...] = h_hi
        if precise:
            scratch[1][...] = (h - h_hi.astype(F32)).astype(BF16)
        if emit_h:
            rest[n_w + 1][...] = h_hi

    dot = functools.partial(jnp.dot, preferred_element_type=F32)
    if precise:
        h_hi, h_lo = scratch[0][...], scratch[1][...]
        y_ref[...] = dot(h_hi, w_refs[0][...]) + (dot(h_hi, w_refs[1][...]) + dot(h_lo, w_refs[0][...]))
    else:
        y_ref[...] = dot(scratch[0][...], w_refs[0][...])


def _split_bf16(w):
    hi = w.astype(BF16)
    return hi, (w - hi.astype(F32)).astype(BF16)


def norm_mod_matmul(x, g, sc, sh, w, *, tm, tn, rows_per_mod, emit_h=False, precise=False):
    t, d = x.shape
    ws = tuple(w) if precise else (w,)
    n = ws[0].shape[1]
    r = sc.shape[1]
    tiles_per_mod = rows_per_mod // tm
    mod_spec = pl.BlockSpec((1, r, d), lambda i, j: (i // tiles_per_mod, 0, 0))
    out_shape = [jax.ShapeDtypeStruct((t, n), F32)]
    out_specs = [pl.BlockSpec((tm, tn), lambda i, j: (i, j))]
    if emit_h:
        out_shape.append(jax.ShapeDtypeStruct((t, d), BF16))
        out_specs.append(pl.BlockSpec((tm, d), lambda i, j: (i, 0)))
    res = pl.pallas_call(
        functools.partial(_nmm_kernel, emit_h=emit_h, precise=precise),
        grid=(t // tm, n // tn),
        in_specs=[pl.BlockSpec((tm, d), lambda i, j: (i, 0)),
                  pl.BlockSpec((1, d), lambda i, j: (0, 0)),
                  mod_spec, mod_spec] + [pl.BlockSpec((d, tn), lambda i, j: (0, j))] * len(ws),
        out_specs=out_specs,
        out_shape=out_shape,
        scratch_shapes=[pltpu.VMEM((tm, d), BF16)] * len(ws),
        compiler_params=_params(48, 2),
        name="norm_mod_matmul",
    )(x, g.reshape(1, d), sc, sh, *ws)
    return res if emit_h else res[0]


def _rope(x, cos_full, sin_signed):
    return x * cos_full + pltpu.roll(x, 64, 1) * sin_signed


def _ret_gamma_logs():
    return np.log(1.0 - np.exp2(-5.0 - np.arange(H_RET, dtype=np.float64)))


def _dot_t(a, b):
    return lax.dot_general(a, b, (((0,), (0,)), ((), ())), preferred_element_type=F32)


def _dot_nt(a, b, **kw):
    return lax.dot_general(a, b, (((1,), (1,)), ((), ())), preferred_element_type=F32, **kw)


def _ret_head_qkv(q_ref, k_ref, v_ref, cos, sin, h):
    sl = slice(h * 128, (h + 1) * 128)
    q = _rope(q_ref[0, :, sl], cos, sin)
    k = _rope(k_ref[0, :, sl], cos, sin) * (DK_RET ** -0.5)
    v = v_ref[0, :, sl]
    return q, k, v


def _ret_finish(o, gate, g_row):
    mu = jnp.mean(o, axis=-1, keepdims=True)
    xc = o - mu
    var = jnp.mean(xc * xc, axis=-1, keepdims=True)
    y = xc * lax.rsqrt(var + EPS) * g_row
    return gate * jax.nn.sigmoid(gate) * y


def _ret_prompt_kernel(q_ref, k_ref, v_ref, g_ref, cos_ref, sin_ref, dm_ref, qd_ref, kd_ref, gr_ref,
                       o_ref, sfin_ref, state, *, chunk_dec, n_chunks):
    c = pl.program_id(1)

    @pl.when(c == 0)
    def _():
        state[...] = jnp.zeros_like(state)

    cos = cos_ref[...]
    sin = sin_ref[...]
    for h in range(H_RET):
        sl = slice(h * 128, (h + 1) * 128)
        q, k, v = _ret_head_qkv(q_ref, k_ref, v_ref, cos, sin, h)
        vb = v.astype(BF16)
        scores = _dot_nt(q.astype(BF16), k.astype(BF16)) * dm_ref[h]
        s_prev = state[h]
        o = jnp.dot(scores.astype(BF16), vb, preferred_element_type=F32)
        o = o + jnp.dot((q * qd_ref[:, sl]).astype(BF16), s_prev.astype(BF16), preferred_element_type=F32)
        state[h] = chunk_dec[h] * s_prev + _dot_t((k * kd_ref[:, sl]).astype(BF16), vb)
        o_ref[0, :, sl] = _ret_finish(o, g_ref[0, :, sl], gr_ref[:, sl]).astype(o_ref.dtype)

    @pl.when(c == n_chunks - 1)
    def _():
        sfin_ref[0] = state[...]


def _ret_tables(cs, rows):
    lg = _ret_gamma_logs()
    i = np.arange(rows) % cs
    seq = np.arange(rows) // cs
    diff = i[:, None] - i[None, :]
    same = seq[:, None] == seq[None, :]
    dmask = np.where((diff >= 0)[None] & same[None], np.exp(diff[None] * lg[:, None, None]), 0.0)
    k_dec = np.exp((cs - 1 - i)[:, None] * lg[None, :])
    q_dec = np.exp((i + 1)[:, None] * lg[None, :])
    k_dec = np.repeat(k_dec, 128, axis=1)
    q_dec = np.repeat(q_dec, 128, axis=1)
    chunk_dec = tuple(float(x) for x in np.exp(cs * lg))
    return (jnp.asarray(dmask, F32), jnp.asarray(q_dec, F32), jnp.asarray(k_dec, F32), chunk_dec)


def _rope_tables(pos):
    half = DK_RET // 2
    inv = ROPE_BASE ** (-jnp.arange(half, dtype=F32) / half)
    ang = pos.astype(F32)[:, None] * inv[None, :]
    cos = jnp.cos(ang)
    sin = jnp.sin(ang)
    return jnp.concatenate([cos, cos], axis=-1), jnp.concatenate([-sin, sin], axis=-1)


def retention_prompt(y3, rope_cos, rope_sin, g_ret):
    b, l, _ = y3.shape
    cs = RET_CHUNK
    nc = l // cs
    dmask, q_dec, k_dec, chunk_dec = _ret_tables(cs, cs)

    def col(cb):
        return pl.BlockSpec((1, cs, 512), lambda bi, c: (bi, c, cb))

    const2 = lambda shape: pl.BlockSpec(shape, lambda bi, c: (0,) * len(shape))
    return pl.pallas_call(
        functools.partial(_ret_prompt_kernel, chunk_dec=chunk_dec, n_chunks=nc),
        grid=(b, nc),
        in_specs=[col(CB_RQ), col(CB_RK), col(CB_RV), col(CB_RG),
                  pl.BlockSpec((cs, 128), lambda bi, c: (c, 0)),
                  pl.BlockSpec((cs, 128), lambda bi, c: (c, 0)),
                  const2((H_RET, cs, cs)), const2((cs, 512)), const2((cs, 512)), const2((1, 512))],
        out_specs=[pl.BlockSpec((1, cs, 512), lambda bi, c: (bi, c, 0)),
                   pl.BlockSpec((1, H_RET, 128, 128), lambda bi, c: (bi, 0, 0, 0))],
        out_shape=[jax.ShapeDtypeStruct((b, l, 512), BF16),
                   jax.ShapeDtypeStruct((b, H_RET, 128, 128), F32)],
        scratch_shapes=[pltpu.VMEM((H_RET, 128, 128), F32)],
        compiler_params=_params(32, 2),
        name="retention_prompt",
    )(y3, y3, y3, y3, rope_cos, rope_sin, dmask, q_dec, k_dec, g_ret.reshape(1, 512))


def _ret_sample_kernel(q_ref, k_ref, v_ref, g_ref, cos_ref, sin_ref, dm_ref, qd_ref, kd_ref, gr_ref, s0_ref,
                       o_ref, sfin_ref, o_state, *, chunk_dec, n_seq, cs):
    b = pl.program_id(0)

    @pl.when(b == 0)
    def _():
        o_state[...] = jnp.zeros_like(o_state)

    cos = cos_ref[...]
    sin = sin_ref[...]
    rows = q_ref.shape[1]
    in_seq = lax.shift_right_logical(lax.broadcasted_iota(jnp.int32, (rows, 128), 0), cs.bit_length() - 1) == b
    for h in range(H_RET):
        sl = slice(h * 128, (h + 1) * 128)
        q, k, v = _ret_head_qkv(q_ref, k_ref, v_ref, cos, sin, h)
        s_prev = s0_ref[0, 0, h]
        qm = jnp.where(in_seq, q * qd_ref[:, sl], 0.0)
        km = jnp.where(in_seq, k * kd_ref[:, sl], 0.0)
        o_state[:, sl] += jnp.dot(qm.astype(BF16), s_prev.astype(BF16), preferred_element_type=F32)
        sfin_ref[0, h] = chunk_dec[h] * s_prev + _dot_t(km.astype(BF16), v.astype(BF16))

    @pl.when(b == n_seq - 1)
    def _():
        for h in range(H_RET):
            sl = slice(h * 128, (h + 1) * 128)
            q, k, v = _ret_head_qkv(q_ref, k_ref, v_ref, cos, sin, h)
            scores = _dot_nt(q.astype(BF16), k.astype(BF16)) * dm_ref[h]
            o = jnp.dot(scores.astype(BF16), v.astype(BF16), preferred_element_type=F32) + o_state[:, sl]
            o_ref[0, :, sl] = _ret_finish(o, g_ref[0, :, sl], gr_ref[:, sl]).astype(o_ref.dtype)


def retention_sample(y3, rope_cos, rope_sin, g_ret, state_ret, layer, cs):
    _, t, _ = y3.shape
    n_seq = t // cs
    dmask, q_dec, k_dec, chunk_dec = _ret_tables(cs, t)

    def col(cb):
        return pl.BlockSpec((1, t, 512), lambda bi: (0, 0, cb))

    const = lambda shape: pl.BlockSpec(shape, lambda bi: (0,) * len(shape))
    return pl.pallas_call(
        functools.partial(_ret_sample_kernel, chunk_dec=chunk_dec, n_seq=n_seq, cs=cs),
        grid=(n_seq,),
        in_specs=[col(CB_RQ), col(CB_RK), col(CB_RV), col(CB_RG),
                  const((t, 128)), const((t, 128)),
                  const((H_RET, t, t)), const((t, 512)), const((t, 512)), const((1, 512)),
                  pl.BlockSpec((1, 1, H_RET, 128, 128), lambda bi: (bi, layer, 0, 0, 0))],
        out_specs=[pl.BlockSpec((1, t, 512), lambda bi: (0, 0, 0)),
                   pl.BlockSpec((1, H_RET, 128, 128), lambda bi: (bi, 0, 0, 0))],
        out_shape=[jax.ShapeDtypeStruct((1, t, 512), BF16),
                   jax.ShapeDtypeStruct((n_seq, H_RET, 128, 128), F32)],
        scratch_shapes=[pltpu.VMEM((t, 512), F32)],
        compiler_params=_params(32, 1),
        name="retention_sample",
    )(y3, y3, y3, y3, rope_cos, rope_sin, dmask, q_dec, k_dec, g_ret.reshape(1, 512), state_ret)


def _sgu_kernel(u_ref, v_ref, g_ref, w_ref, b_ref, o_ref, *rest, emit_vn):
    v = v_ref[0]
    ms = jnp.mean(v * v, axis=-1, keepdims=True)
    vn = v * lax.rsqrt(ms + EPS) * g_ref[...]
    if emit_vn:
        rest[0][0] = vn
    vb = vn.astype(BF16)
    for g in range(G_SGU):
        sl = slice(g * 128, (g + 1) * 128)
        mixed = jnp.dot(w_ref[g], vb[:, sl], preferred_element_type=F32) + b_ref[g]
        o_ref[0, :, sl] = (u_ref[0, :, sl] * mixed).astype(o_ref.dtype)


def spatial_gating(y3, g_sgu, w_mix, b_mix, emit_vn):
    b, l, _ = y3.shape
    rows = SGU_CHUNK
    nc = l // rows

    def col(cb):
        return pl.BlockSpec((1, rows, 512), lambda bi, c: (bi, c, cb))

    const = lambda shape: pl.BlockSpec(shape, lambda bi, c: (0,) * len(shape))
    out_spec = pl.BlockSpec((1, rows, 512), lambda bi, c: (bi, c, 0))
    out_shape = [jax.ShapeDtypeStruct((b, l, 512), BF16)]
    out_specs = [out_spec]
    if emit_vn:
        out_shape.append(jax.ShapeDtypeStruct((b, l, 512), F32))
        out_specs.append(out_spec)
    res = pl.pallas_call(
        functools.partial(_sgu_kernel, emit_vn=emit_vn),
        grid=(b, nc),
        in_specs=[col(CB_SU), col(CB_SV), const((1, 512)),
                  const((G_SGU, rows, rows)), const((G_SGU, rows, 128))],
        out_specs=out_specs, out_shape=out_shape,
        compiler_params=_params(32, 2),
        name="spatial_gating",
    )(y3, y3, g_sgu.reshape(1, 512), w_mix, b_mix)
    return res if emit_vn else (res[0], None)


def _qk_prep_kernel(q_ref, k_ref, v_ref, gq_ref, gk_ref, qn_ref, kn_ref, knb_ref, vb_ref):
    for h in range(H_SB):
        sl = slice(h * 128, (h + 1) * 128)
        q = q_ref[:, sl]
        k = k_ref[:, sl]
        qn = q * lax.rsqrt(jnp.mean(q * q, axis=-1, keepdims=True) + EPS) * gq_ref[...]
        kn = k * lax.rsqrt(jnp.mean(k * k, axis=-1, keepdims=True) + EPS) * gk_ref[...]
        qn_ref[:, sl] = (qn * Q_LOGIT_SCALE).astype(BF16)
        kn_ref[:, sl] = kn
        knb_ref[:, sl] = kn.astype(BF16)
    vb_ref[...] = v_ref[...].astype(BF16)


def qk_prep(y, g_q, g_k, tm):
    t = y.shape[0]

    def col(cb):
        return pl.BlockSpec((tm, 512), lambda i: (i, cb))

    out_spec = pl.BlockSpec((tm, 512), lambda i: (i, 0))
    return pl.pallas_call(
        _qk_prep_kernel,
        grid=(t // tm,),
        in_specs=[col(CB_BQ), col(CB_BK), col(CB_BV),
                  pl.BlockSpec((1, 128), lambda i: (0, 0)), pl.BlockSpec((1, 128), lambda i: (0, 0))],
        out_specs=[out_spec] * 4,
        out_shape=[jax.ShapeDtypeStruct((t, 512), BF16), jax.ShapeDtypeStruct((t, 512), F32),
                   jax.ShapeDtypeStruct((t, 512), BF16), jax.ShapeDtypeStruct((t, 512), BF16)],
        compiler_params=_params(32, 1),
        name="qk_prep",
    )(y, y, y, g_q.reshape(1, 128), g_k.reshape(1, 128))


def _cumsum_weights():
    j = np.arange(128)
    later = (j[:, None] > j[None, :]).astype(np.float32)
    w = np.concatenate([later, np.ones((128, 128), np.float32)], axis=1)
    return jnp.asarray(np.concatenate([w, w], axis=0), BF16)


def _sb_tile(z2, carry, uo, mask):
    base, r = _sb_stage1(z2, uo, mask)
    return _sb_stage2(base, r, carry)


MASKED_LOG = -1e30


def _sb_stage1(z2, uo, mask):
    lg2 = jnp.log(1.0 + jnp.exp2(-jnp.abs(z2))) * LOG2E
    log_1m = -(jnp.maximum(z2, 0.0) + lg2)
    base = z2 + log_1m
    if mask is not None:
        log_1m = jnp.where(mask, log_1m, 0.0)
        base = jnp.where(mask, base, MASKED_LOG)
    hi = log_1m.astype(BF16)
    lo = (log_1m - hi.astype(F32)).astype(BF16)
    r = jnp.dot(jnp.concatenate([hi, lo], axis=1), uo, preferred_element_type=F32)
    return base, r


def _sb_stage2(base, r, carry):
    a = jnp.exp2(base + (r[:, :128] + carry))
    return a, carry + r[:, 128:]


def _attn_prompt_kernel(bias_ref, q_ref, k_ref, v_ref, uo_ref, o_ref, *, tq):
    h = pl.program_id(1)
    i = pl.program_id(2)
    q = q_ref[0]
    bias2 = bias_ref[h] * LOG2E
    uo = uo_ref[...]
    n_band = tq // 128
    row = lax.broadcasted_iota(jnp.int32, (tq, 128), 0)
    colk = lax.broadcasted_iota(jnp.int32, (tq, 128), 1)

    def tile_start(t):
        return pl.multiple_of(jnp.maximum((i + 1) * n_band - 1 - t, 0) * 128, 128)

    def logits(t):
        return _dot_nt(q, k_ref[0, pl.ds(tile_start(t), 128), :]) + bias2

    def finish(t_prev, acc, log_a_prev):
        v = v_ref[0, pl.ds(tile_start(t_prev), 128), :]
        return acc + jnp.dot(jnp.exp2(log_a_prev).astype(BF16), v, preferred_element_type=F32)

    def trip(t, c, mask):
        carry, acc, z2, log_a_prev = c
        z_next = logits(t + 1)
        acc = finish(jnp.maximum(t - 1, 0), acc, log_a_prev)
        base, r = _sb_stage1(z2, uo, mask)
        return carry + r[:, 128:], acc, z_next, base + (r[:, :128] + carry)

    zero = jnp.zeros((tq, 128), F32)
    c = (zero, zero, logits(0), jnp.full((tq, 128), MASKED_LOG, F32))
    for t in range(n_band):
        c = trip(t, c, colk + (n_band - 1 - t) * 128 < row)
    c = lax.fori_loop(n_band, (i + 1) * n_band, lambda t, c: trip(t, c, None), c)
    acc = finish((i + 1) * n_band - 1, c[1], c[3])
    o_ref[0] = acc.astype(o_ref.dtype)


def attention_prompt(qn, knb, vb, b_sb, uo, tq=512):
    b, l, _ = qn.shape
    tq = min(tq, l)
    return pl.pallas_call(
        functools.partial(_attn_prompt_kernel, tq=tq),
        grid=(b, H_SB, l // tq),
        in_specs=[pl.BlockSpec(memory_space=pltpu.SMEM),
                  pl.BlockSpec((1, tq, 128), lambda bi, h, i: (bi, i, h)),
                  pl.BlockSpec((1, l, 128), lambda bi, h, i: (bi, 0, h)),
                  pl.BlockSpec((1, l, 128), lambda bi, h, i: (bi, 0, h)),
                  pl.BlockSpec((256, 256), lambda bi, h, i: (0, 0))],
        out_specs=pl.BlockSpec((1, tq, 128), lambda bi, h, i: (bi, i, h)),
        out_shape=jax.ShapeDtypeStruct((b, l, 512), BF16),
        compiler_params=_params(32, 3),
        name="attention_prompt",
    )(b_sb, qn, knb, vb, uo)


ROWS_PER_HEAD = 8


def _attn_sample_kernel(pt_ref, q_ref, kn_ref, vn_ref, brow_ref, uo_ref, *rest, n_slots, n_steps):
    kp = rest[:n_slots]
    vp = rest[n_slots:2 * n_slots]
    o_ref, carry_s, acc_s = rest[2 * n_slots:]
    j = pl.program_id(1)
    uo = uo_ref[...]
    brow = brow_ref[...]
    heads = range(H_SB)
    rows = [slice(h * ROWS_PER_HEAD, (h + 1) * ROWS_PER_HEAD) for h in heads]
    qs = [q_ref[0, rows[h], :] for h in heads]

    def logits(k_of_head):
        return jnp.concatenate([_dot_nt(qs[h], k_of_head(h)) for h in heads], axis=0) + brow

    def accumulate(a, v_of_head):
        a = a.astype(BF16)
        for h in heads:
            acc_s[rows[h], :] += jnp.dot(a[rows[h]], v_of_head(h), preferred_element_type=F32)

    @pl.when(j == 0)
    def _():
        acc_s[...] = jnp.zeros_like(acc_s)
        m = brow.shape[0]
        tok = lax.broadcasted_iota(jnp.int32, (m, 128), 0) & (ROWS_PER_HEAD - 1)
        key = lax.broadcasted_iota(jnp.int32, (m, 128), 1)
        a, carry = _sb_tile(logits(lambda h: kn_ref[0, h]), jnp.zeros((m, 128), F32), uo, key < tok)
        carry_s[...] = carry
        accumulate(a, lambda h: vn_ref[0, h])

    def page_head(ref, h):
        return ref[0, 0, pl.ds(h, 128, stride=H_SB), :].astype(BF16)

    stage1 = [_sb_stage1(logits(lambda h, s=s: page_head(kp[s], h)), uo, None) for s in range(n_slots)]
    carry = carry_s[...]
    for s in range(n_slots):
        a, carry = _sb_stage2(*stage1[s], carry)
        accumulate(a, lambda h, s=s: page_head(vp[s], h))
    carry_s[...] = carry

    @pl.when(j == n_steps - 1)
    def _():
        o_ref[0] = acc_s[...]


def attention_sample(q8, k_new, v_new, brow, uo, cache_k, cache_v, page_table, layer, n_slots=4):
    n_seq, n_pages = page_table.shape
    n_steps = n_pages // n_slots
    m = q8.shape[1]
    page_rows = cache_k.shape[2]

    def page_spec(s):
        return pl.BlockSpec((1, 1, page_rows, 128),
                            lambda bi, j, pt: (pt[bi, n_pages - 1 - (j * n_slots + s)], layer, 0, 0))

    per_seq = lambda shape: pl.BlockSpec(shape, lambda bi, j, pt: (bi,) + (0,) * (len(shape) - 1))
    const = lambda shape: pl.BlockSpec(shape, lambda bi, j, pt: (0,) * len(shape))
    grid_spec = pltpu.PrefetchScalarGridSpec(
        num_scalar_prefetch=1,
        grid=(n_seq, n_steps),
        in_specs=[per_seq((1, m, 128)), per_seq((1, H_SB, 128, 128)), per_seq((1, H_SB, 128, 128)),
                  const((m, 128)), const((256, 256))]
                 + [page_spec(s) for s in range(n_slots)] * 2,
        out_specs=per_seq((1, m, 128)),
        scratch_shapes=[pltpu.VMEM((m, 128), F32), pltpu.VMEM((m, 128), F32)],
    )
    return pl.pallas_call(
        functools.partial(_attn_sample_kernel, n_slots=n_slots, n_steps=n_steps),
        grid_spec=grid_spec,
        out_shape=jax.ShapeDtypeStruct((n_seq, m, 128), F32),
        compiler_params=_params(32, 2),
        name="attention_sample",
    )(page_table, q8, k_new, v_new, brow, uo, *([cache_k] * n_slots), *([cache_v] * n_slots))


def _merge_kernel(x_ref, oa_ref, ob_ref, oc_ref, gates_ref, wb_ref, wo_ref, ga_ref, o_ref):
    merged = None
    for br, o_b in enumerate((oa_ref, ob_ref, oc_ref)):
        gate = jax.nn.sigmoid(gates_ref[:, br * D_MODEL:(br + 1) * D_MODEL])
        term = gate * jnp.dot(o_b[...], wb_ref[br], preferred_element_type=F32)
        merged = term if merged is None else merged + term
    y = jnp.dot(merged.astype(BF16), wo_ref[...], preferred_element_type=F32)
    o_ref[...] = x_ref[...] + ga_ref[0] * y


def merge_out(x, o_ret, o_sgu, o_sb, y, w_branch, w_out, ga, *, tm, rows_per_mod):
    t, d = x.shape
    r = ga.shape[1]
    tiles_per_mod = rows_per_mod // tm
    row = lambda w: pl.BlockSpec((tm, w), lambda i: (i, 0))
    return pl.pallas_call(
        _merge_kernel,
        grid=(t // tm,),
        in_specs=[row(d), row(512), row(512), row(512), row(N_GATE_COLS),
                  pl.BlockSpec((3, BRANCH_W, d), lambda i: (0, 0, 0)),
                  pl.BlockSpec((d, d), lambda i: (0, 0)),
                  pl.BlockSpec((1, r, d), lambda i: (i // tiles_per_mod, 0, 0))],
        out_specs=row(d),
        out_shape=jax.ShapeDtypeStruct((t, d), F32),
        compiler_params=_params(48, 1),
        name="merge_out",
    )(x, o_ret, o_sgu, o_sb, y, w_branch, w_out, ga)


def _topk_rows(vals, ids, k):
    rank = jnp.full(vals.shape, NOT_RANKED, F32)
    tops = []
    for a in range(k):
        m = jnp.max(vals, axis=0, keepdims=True)
        first = jnp.min(jnp.where(vals == m, ids, 1e9), axis=0, keepdims=True)
        hit = ids == first
        rank = jnp.where(hit, float(a), rank)
        vals = jnp.where(hit, -jnp.inf, vals)
        tops.append(m)
    return tops, rank, vals


def _peer_select_chunk(s1, s2):
    tt = s1.shape[1]
    key_id = lax.broadcasted_iota(jnp.int32, (PEER_NKEYS, tt), 0).astype(F32)
    top1, rank1, _ = _topk_rows(s1, key_id, PEER_TOPK)
    top2_rows, rank2, _ = _topk_rows(s2, key_id, PEER_TOPK)
    top_id = lax.broadcasted_iota(jnp.int32, (PEER_TOPK, tt), 0)
    t1 = jnp.zeros((PEER_TOPK, tt), F32)
    t2 = jnp.zeros((PEER_TOPK, tt), F32)
    for r in range(PEER_TOPK):
        t1 = jnp.where(top_id == r, top1[r], t1)
        t2 = jnp.where(top_id == r, top2_rows[r], t2)
    half = PEER_TOPK // 2
    rank_id = top_id.astype(F32)
    slabs = [t1 + top2_rows[0], top1[0] + t2[half:]]
    ids = [rank_id * PEER_TOPK, rank_id[half:]]
    for b in range(1, half):
        slabs.append(t1[:half] + top2_rows[b])
        ids.append(rank_id[:half] * PEER_TOPK + float(b))
    _, _, left = _topk_rows(jnp.concatenate(slabs, axis=0), jnp.concatenate(ids, axis=0), PEER_TOPK)
    chosen = jnp.where(left == -jnp.inf, 1.0, 0.0)
    ch_b0 = chosen[:PEER_TOPK]
    ch_a0 = chosen[PEER_TOPK:PEER_TOPK + half]
    e1_top = jnp.exp(t1 - top1[0])
    e2_top = jnp.exp(t2 - top2_rows[0])
    inner_cnt = jnp.zeros((half, tt), F32)
    inner_e2 = jnp.zeros((half, tt), F32)
    for b in range(1, half):
        ch = chosen[PEER_TOPK + b * half:PEER_TOPK + (b + 1) * half]
        inner_cnt = inner_cnt + ch
        inner_e2 = inner_e2 + ch * e2_top[b:b + 1]
    tail_cnt = jnp.sum(ch_a0, axis=0, keepdims=True)
    z = (jnp.sum(ch_b0 * e1_top, axis=0, keepdims=True) + jnp.sum(ch_a0 * e2_top[half:], axis=0, keepdims=True)
         + jnp.sum(inner_e2 * e1_top[:half], axis=0, keepdims=True))
    counts = [ch_b0[a:a + 1] + (inner_cnt[a:a + 1] if a < half else 0.0) + (tail_cnt if a == 0 else 0.0)
              for a in range(PEER_TOPK)]
    c1 = jnp.zeros((PEER_NKEYS, tt), F32)
    for a in range(PEER_TOPK):
        c1 = jnp.where(rank1 == float(a), counts[a], c1)
    f1 = jnp.exp(s1 - top1[0]) / z
    return c1, f1, rank2.astype(BF16), jnp.exp(s2 - top2_rows[0]).astype(BF16)


def _peer_select_kernel(q_ref, sk_ref, c1_ref, f1_ref, r2_ref, f2_ref):
    tt = q_ref.shape[0]
    grouped = (PEER_NKEYS // BF16_ROWS, BF16_ROWS, LANES)
    for c in range(tt // LANES):
        tok = slice(c * LANES, (c + 1) * LANES)
        s1, s2 = [_dot_nt(sk_ref[p], q_ref[tok, p * 128:(p + 1) * 128], precision=lax.Precision.HIGHEST)
                  for p in range(2)]
        c1, f1, r2, f2 = _peer_select_chunk(s1, s2)
        c1_ref[0, :, tok] = c1
        f1_ref[0, :, tok] = f1
        r2_ref[0, :, :, tok] = r2.reshape(grouped)
        f2_ref[0, :, :, tok] = f2.reshape(grouped)


def peer_select(q, sub_keys, tt):
    t = q.shape[0]
    n_groups = PEER_NKEYS // BF16_ROWS
    shape1 = jax.ShapeDtypeStruct((PEER_HEADS, PEER_NKEYS, t), F32)
    shape2 = jax.ShapeDtypeStruct((PEER_HEADS, n_groups, BF16_ROWS, t), BF16)
    spec1 = pl.BlockSpec((1, PEER_NKEYS, tt), lambda i, h: (h, 0, i))
    spec2 = pl.BlockSpec((1, n_groups, BF16_ROWS, tt), lambda i, h: (h, 0, 0, i))
    return pl.pallas_call(
        _peer_select_kernel,
        grid=(t // tt, PEER_HEADS),
        in_specs=[pl.BlockSpec((tt, 256), lambda i, h: (i, h)),
                  pl.BlockSpec((2, PEER_NKEYS, 128), lambda i, h: (0, 0, 0))],
        out_specs=[spec1, spec1, spec2, spec2], out_shape=[shape1, shape1, shape2, shape2],
        compiler_params=_params(32, 2),
        name="peer_select",
    )(q, sub_keys)


def _gelu_tanh(x):
    k = -2.0 * 0.7978845608028654 * LOG2E
    return x / (1.0 + jnp.exp2(x * (k + (k * 0.044715) * (x * x))))


def _peer_dense_kernel(h_ref, u_ref, vt_ref, c1_ref, f1_ref, r2_ref, f2_ref, x_ref, ga_ref, o_ref,
                       acc, act, wact, *, n_rows, n_steps):
    e = pl.program_id(1)
    tt = h_ref.shape[0]

    @pl.when(e == 0)
    def _():
        acc[...] = jnp.zeros_like(acc)

    chunk = min(tt, 2 * LANES)
    zero = jnp.zeros((BF16_ROWS, chunk), BF16)
    n_groups = PEER_NKEYS // BF16_ROWS
    te = u_ref.shape[0]
    act[...] = _gelu_tanh(_dot_nt(u_ref[...], h_ref[...])).astype(BF16).reshape(act.shape)

    for il in range(n_rows):
        for tc in range(tt // chunk):
            lanes = slice(tc * chunk, (tc + 1) * chunk)
            w = [zero] * n_groups
            for h in range(PEER_HEADS):
                c1 = jnp.broadcast_to(c1_ref[h, il:il + 1, lanes], (BF16_ROWS, chunk)).astype(BF16)
                f1 = jnp.broadcast_to(f1_ref[h, il:il + 1, lanes], (BF16_ROWS, chunk)).astype(BF16)
                for g in range(n_groups):
                    w[g] = w[g] + jnp.where(r2_ref[h, g, :, lanes] < c1, f2_ref[h, g, :, lanes], zero) * f1
            for g in range(n_groups):
                wact[il * n_groups + g, :, lanes] = w[g] * act[il * n_groups + g, :, lanes]
    acc[...] += jnp.dot(vt_ref[...], wact[...].reshape(te, tt), preferred_element_type=F32)

    @pl.when(e == n_steps - 1)
    def _():
        o_ref[...] = x_ref[...] + ga_ref[0] * acc[...].T


def peer_dense(x, h2, exp_u, exp_vt, sel, ga, *, tt, rows_per_mod, n_rows=8):
    t, d = x.shape
    n_exp = exp_u.shape[0]
    te = n_rows * PEER_NKEYS
    n_steps = n_exp // te
    r = ga.shape[1]
    tiles_per_mod = rows_per_mod // tt
    row_spec = pl.BlockSpec((PEER_HEADS, n_rows, tt), lambda i, e: (0, e, i))
    n_groups = PEER_NKEYS // BF16_ROWS
    full_spec = pl.BlockSpec((PEER_HEADS, n_groups, BF16_ROWS, tt), lambda i, e: (0, 0, 0, i))
    grouped = pltpu.VMEM((te // BF16_ROWS, BF16_ROWS, tt), BF16)
    return pl.pallas_call(
        functools.partial(_peer_dense_kernel, n_rows=n_rows, n_steps=n_steps),
        grid=(t // tt, n_steps),
        in_specs=[pl.BlockSpec((tt, d), lambda i, e: (i, 0)),
                  pl.BlockSpec((te, d), lambda i, e: (e, 0)),
                  pl.BlockSpec((d, te), lambda i, e: (0, e)),
                  row_spec, row_spec, full_spec, full_spec,
                  pl.BlockSpec((tt, d), lambda i, e: (i, 0)),
                  pl.BlockSpec((1, r, d), lambda i, e: (i // tiles_per_mod, 0, 0))],
        out_specs=pl.BlockSpec((tt, d), lambda i, e: (i, 0)),
        out_shape=jax.ShapeDtypeStruct((t, d), F32),
        scratch_shapes=[pltpu.VMEM((d, tt), F32), grouped, grouped],
        compiler_params=_params(56, 2),
        name="peer_dense",
    )(h2, exp_u, exp_vt, *sel, x, ga)


def _token_tile(rows_per_mod):
    return 512 if rows_per_mod % 512 == 0 else rows_per_mod


def _layer(x, mods, lw, *, n_batch, sample):
    t, d = x.shape
    l = t // n_batch
    sh1, sc1, ga1, sh2, sc2, ga2 = mods
    rows_per_mod = l if sample is None else t
    tm = _token_tile(rows_per_mod)
    tm_in = 2 * tm if rows_per_mod % (2 * tm) == 0 else tm
    y = norm_mod_matmul(x, lw["g_norm1"], sc1, sh1, lw["w_in"], tm=tm_in, tn=1536, rows_per_mod=rows_per_mod)

    qn, kn, knb, vb = qk_prep(y, lw["g_q"], lw["g_k"], tm)
    if sample is None:
        y3 = y.reshape(n_batch, l, -1)
        o_ret, s_new = retention_prompt(y3, lw["rope_cos"], lw["rope_sin"], lw["g_ret"])
        o_sgu, vn_rows = spatial_gating(y3, lw["g_sgu"], lw["w_mix"], lw["b_mix"], emit_vn=False)
        o_sb = attention_prompt(qn.reshape(n_batch, l, 512), knb.reshape(n_batch, l, 512),
                                vb.reshape(n_batch, l, 512), lw["b_sb"], lw["uo"])
    else:
        y3 = y.reshape(1, t, -1)
        o_ret, s_new = retention_sample(y3, lw["rope_cos"], lw["rope_sin"], lw["g_ret"],
                                        sample["state_ret"], sample["layer"], l)
        o_sgu, vn_rows = spatial_gating(y3, lw["g_sgu"], lw["w_mix"], lw["b_mix"], emit_vn=True)
        def head_major(a, rows):
            a = jnp.transpose(a.reshape(n_batch, l, H_SB, D_SB), (0, 2, 1, 3))
            return jnp.pad(a, ((0, 0), (0, 0), (0, rows - l), (0, 0)))

        q8 = head_major(qn, ROWS_PER_HEAD).reshape(n_batch, H_SB * ROWS_PER_HEAD, D_SB)
        brow = jnp.broadcast_to(jnp.repeat(lw["b_sb"] * LOG2E, ROWS_PER_HEAD)[:, None],
                                (H_SB * ROWS_PER_HEAD, 128))
        o8 = attention_sample(q8, head_major(knb, 128), head_major(vb, 128), brow, lw["uo"],
                              sample["cache_k"], sample["cache_v"], sample["page_table"], sample["layer"])
        o8 = o8.reshape(n_batch, H_SB, ROWS_PER_HEAD, D_SB)[:, :, :l]
        o_sb = jnp.transpose(o8, (0, 2, 1, 3)).reshape(t, 512).astype(BF16)

    x = merge_out(x, o_ret.reshape(t, 512), o_sgu.reshape(t, 512), o_sb.reshape(t, 512), y,
                  lw["w_branch"], lw["w_out"], ga1, tm=tm, rows_per_mod=rows_per_mod)

    pq, h2 = norm_mod_matmul(x, lw["g_norm2"], sc2, sh2, lw["w_pq"], tm=tm, tn=512,
                             rows_per_mod=rows_per_mod, emit_h=True, precise=True)
    tt = tm
    sel = peer_select(pq, lw["sub_keys"], tt)
    x = peer_dense(x, h2, lw["exp_u"], lw["exp_vt"], sel, ga2, tt=tt, rows_per_mod=rows_per_mod)
    v_raw = y[:, CB_BV * 512:(CB_BV + 1) * 512]
    return x, s_new, kn, v_raw, vn_rows


def kernel(x_prompt, x_sample, cache_k, cache_v, state_ret, page_table, c_prompt, c_sample, w_ada, b_ada,
           g_norm1, g_norm2, w_in, g_ret, g_q, g_k, b_sb, g_sgu, w_sp, b_sp, w_branch, w_out, w_pq,
           sub_keys, exp_u, exp_v):
    n_b, seq, d = x_prompt.shape
    n_s, dec, _ = x_sample.shape
    depth = w_in.shape[0]
    n_pool, _, page, _, _ = cache_k.shape
    past_len = page_table.shape[1] * page
    t_s = n_s * dec

    mod = ada_mod(jnp.concatenate([c_prompt, c_sample], axis=0), w_ada, b_ada)
    cache_k4 = cache_k.reshape(n_pool, depth, page * H_SB, D_SB)
    cache_v4 = cache_v.reshape(n_pool, depth, page * H_SB, D_SB)
    uo = _cumsum_weights()
    cos_p, sin_p = _rope_tables(jnp.arange(seq))
    cos_s, sin_s = _rope_tables(past_len + (jnp.arange(t_s) % dec))

    pos = np.arange(t_s)
    same_seq = jnp.asarray((pos[:, None] // dec) == (pos[None, :] // dec))
    tri = jnp.asarray(np.tril(np.ones((SGU_CHUNK, SGU_CHUNK), bool)))

    xp = x_prompt.reshape(n_b * seq, d)
    xs = x_sample.reshape(t_s, d)
    outs = {k: [] for k in ("kp", "vp", "ks", "vs", "rp", "rs", "us")}
    for l in range(depth):
        w_in_l = jnp.concatenate([w_in[l][:, -N_GATE_COLS:], w_in[l][:, :-N_GATE_COLS]], axis=1).astype(BF16)
        shared = dict(g_norm1=g_norm1[l], g_norm2=g_norm2[l], w_in=w_in_l, g_ret=g_ret[l], g_q=g_q[l], g_k=g_k[l],
                      b_sb=b_sb[l], g_sgu=g_sgu[l], w_branch=w_branch[l].astype(BF16), w_out=w_out[l].astype(BF16),
                      w_pq=_split_bf16(w_pq[l]), sub_keys=sub_keys[l], exp_u=exp_u[l].astype(BF16),
                      exp_vt=exp_v[l].T.astype(BF16), uo=uo)
        mods_p = [m.reshape(n_b, 1, d) for m in jnp.split(mod[l, :n_b], 6, axis=-1)]
        mods_s = [jnp.repeat(m, dec, axis=0).reshape(1, t_s, d) for m in jnp.split(mod[l, n_b:], 6, axis=-1)]

        w_tri = jnp.where(tri[None], w_sp[l], 0.0)
        lw_p = dict(shared, rope_cos=cos_p, rope_sin=sin_p, w_mix=w_tri.astype(BF16),
                    b_mix=jnp.broadcast_to(b_sp[l][:, :, None], (G_SGU, SGU_CHUNK, 128)))
        xp, rp, kp, vp, _ = _layer(xp, mods_p, lw_p, n_batch=n_b, sample=None)

        w_blk = jnp.where(same_seq[None], jnp.tile(w_tri[:, :dec, :dec], (1, n_s, n_s)), 0.0)
        b_blk = jnp.tile(b_sp[l][:, :dec], (1, n_s))
        lw_s = dict(shared, rope_cos=cos_s, rope_sin=sin_s, w_mix=w_blk.astype(BF16),
                    b_mix=jnp.broadcast_to(b_blk[:, :, None], (G_SGU, t_s, 128)))
        sample = dict(state_ret=state_ret, cache_k=cache_k4, cache_v=cache_v4, page_table=page_table, layer=l)
        xs, rs, ks, vs, us = _layer(xs, mods_s, lw_s, n_batch=n_s, sample=sample)

        outs["kp"].append(kp.reshape(n_b, seq, H_SB, D_SB))
        outs["vp"].append(vp.reshape(n_b, seq, H_SB, D_SB))
        outs["ks"].append(ks.reshape(n_s, dec, H_SB, D_SB))
        outs["vs"].append(vs.reshape(n_s, dec, H_SB, D_SB))
        outs["rp"].append(rp)
        outs["rs"].append(rs)
        outs["us"].append(us.reshape(n_s, dec, -1))
    stack = lambda name: jnp.stack(outs[name], axis=1)
    return (xp.reshape(n_b, seq, d), xs.reshape(n_s, dec, d),
            stack("kp"), stack("vp"), stack("ks"), stack("vs"), stack("rp"), stack("rs"), stack("us"))
```

```python
import functools

import numpy as np
import jax
import jax.numpy as jnp
from jax import lax
from jax.experimental import pallas as pl
from jax.experimental.pallas import tpu as pltpu

F32 = jnp.float32
BF16 = jnp.bfloat16

D_MODEL = 1024
H_RET = 4
DK_RET = 128
RET_CHUNK = 128
ROPE_BASE = 10000.0
G_SGU = 4
SGU_CHUNK = 128
H_SB = 4
D_SB = 128
BRANCH_W = 512
PEER_HEADS = 8
PEER_NKEYS = 128
PEER_TOPK = 16
EPS = 1e-6

LOG2E = 1.4426950408889634
Q_LOGIT_SCALE = D_SB ** -0.5 * LOG2E
LANES = 128
BF16_ROWS = 16
N_DENSE_PARTS = 4
MIB = 2 ** 20
NOT_RANKED = 99.0
N_GATE_COLS = 3 * D_MODEL
CB_RQ, CB_RK, CB_RV, CB_RG, CB_SU, CB_SV, CB_BQ, CB_BK, CB_BV = range(6, 15)


def _params(vmem_mib, n_grid):
    return pltpu.CompilerParams(dimension_semantics=("arbitrary",) * n_grid,
                                vmem_limit_bytes=vmem_mib * MIB)


def _ada_kernel(c_ref, w_ref, b_ref, o_ref):
    c = c_ref[...]
    s = (c * jax.nn.sigmoid(c)).astype(BF16)
    o_ref[0] = jnp.dot(s, w_ref[0].astype(BF16), preferred_element_type=F32) + b_ref[0]


def ada_mod(c_all, w_ada, b_ada):
    depth, d, n = w_ada.shape
    m = c_all.shape[0]
    tn = 1536
    return pl.pallas_call(
        _ada_kernel,
        grid=(depth, n // tn),
        in_specs=[pl.BlockSpec((m, d), lambda l, j: (0, 0)),
                  pl.BlockSpec((1, d, tn), lambda l, j: (l, 0, j)),
                  pl.BlockSpec((1, 1, tn), lambda l, j: (l, 0, j))],
        out_specs=pl.BlockSpec((1, m, tn), lambda l, j: (l, 0, j)),
        out_shape=jax.ShapeDtypeStruct((depth, m, n), F32),
        compiler_params=_params(40, 2),
        name="ada_mod",
    )(c_all, w_ada, b_ada.reshape(depth, 1, n))


def _nmm_kernel(x_ref, g_ref, sc_ref, sh_ref, *rest, emit_h, precise):
    n_w = 2 if precise else 1
    w_refs, y_ref, scratch = rest[:n_w], rest[n_w], rest[-n_w:]

    @pl.when(pl.program_id(1) == 0)
    def _():
        x = x_ref[...]
        ms = jnp.mean(x * x, axis=-1, keepdims=True)
        y = x * lax.rsqrt(ms + EPS) * g_ref[...]
        h = y * (1.0 + sc_ref[0]) + sh_ref[0]
        h_hi = h.astype(BF16)
        scratch[0][...] = h_hi
        if precise:
            scratch[1][...] = (h - h_hi.astype(F32)).astype(BF16)
        if emit_h:
            rest[n_w + 1][...] = h_hi

    dot = functools.partial(jnp.dot, preferred_element_type=F32)
    if precise:
        h_hi, h_lo = scratch[0][...], scratch[1][...]
        y_ref[...] = dot(h_hi, w_refs[0][...]) + (dot(h_hi, w_refs[1][...]) + dot(h_lo, w_refs[0][...]))
    else:
        y_ref[...] = dot(scratch[0][...], w_refs[0][...])


def _split_bf16(w):
    hi = w.astype(BF16)
    return hi, (w - hi.astype(F32)).astype(BF16)


def norm_mod_matmul(x, g, sc, sh, w, *, tm, tn, rows_per_mod, emit_h=False, precise=False):
    t, d = x.shape
    ws = tuple(w) if precise else (w,)
    n = ws[0].shape[1]
    r = sc.shape[1]
    tiles_per_mod = rows_per_mod // tm
    mod_spec = pl.BlockSpec((1, r, d), lambda i, j: (i // tiles_per_mod, 0, 0))
    out_shape = [jax.ShapeDtypeStruct((t, n), F32)]
    out_specs = [pl.BlockSpec((tm, tn), lambda i, j: (i, j))]
    if emit_h:
        out_shape.append(jax.ShapeDtypeStruct((t, d), BF16))
        out_specs.append(pl.BlockSpec((tm, d), lambda i, j: (i, 0)))
    res = pl.pallas_call(
        functools.partial(_nmm_kernel, emit_h=emit_h, precise=precise),
        grid=(t // tm, n // tn),
        in_specs=[pl.BlockSpec((tm, d), lambda i, j: (i, 0)),
                  pl.BlockSpec((1, d), lambda i, j: (0, 0)),
                  mod_spec, mod_spec] + [pl.BlockSpec((d, tn), lambda i, j: (0, j))] * len(ws),
        out_specs=out_specs,
        out_shape=out_shape,
        scratch_shapes=[pltpu.VMEM((tm, d), BF16)] * len(ws),
        compiler_params=_params(48, 2),
        name="norm_mod_matmul",
    )(x, g.reshape(1, d), sc, sh, *ws)
    return res if emit_h else res[0]


def _rope(x, cos_full, sin_signed):
    return x * cos_full + pltpu.roll(x, 64, 1) * sin_signed


def _ret_gamma_logs():
    return np.log(1.0 - np.exp2(-5.0 - np.arange(H_RET, dtype=np.float64)))


def _dot_t(a, b):
    return lax.dot_general(a, b, (((0,), (0,)), ((), ())), preferred_element_type=F32)


def _dot_nt(a, b, **kw):
    return lax.dot_general(a, b, (((1,), (1,)), ((), ())), preferred_element_type=F32, **kw)


def _ret_head_qkv(q_ref, k_ref, v_ref, cos, sin, h):
    sl = slice(h * 128, (h + 1) * 128)
    q = _rope(q_ref[0, :, sl], cos, sin)
    k = _rope(k_ref[0, :, sl], cos, sin) * (DK_RET ** -0.5)
    v = v_ref[0, :, sl]
    return q, k, v


def _ret_finish(o, gate, g_row):
    mu = jnp.mean(o, axis=-1, keepdims=True)
    xc = o - mu
    var = jnp.mean(xc * xc, axis=-1, keepdims=True)
    y = xc * lax.rsqrt(var + EPS) * g_row
    return gate * jax.nn.sigmoid(gate) * y


def _ret_prompt_kernel(q_ref, k_ref, v_ref, g_ref, cos_ref, sin_ref, dm_ref, qd_ref, kd_ref, gr_ref,
                       o_ref, sfin_ref, state, *, chunk_dec, n_chunks):
    c = pl.program_id(1)

    @pl.when(c == 0)
    def _():
        state[...] = jnp.zeros_like(state)

    cos = cos_ref[...]
    sin = sin_ref[...]
    for h in range(H_RET):
        sl = slice(h * 128, (h + 1) * 128)
        q, k, v = _ret_head_qkv(q_ref, k_ref, v_ref, cos, sin, h)
        vb = v.astype(BF16)
        scores = _dot_nt(q.astype(BF16), k.astype(BF16)) * dm_ref[h]
        s_prev = state[h]
        o = jnp.dot(scores.astype(BF16), vb, preferred_element_type=F32)
        o = o + jnp.dot((q * qd_ref[:, sl]).astype(BF16), s_prev.astype(BF16), preferred_element_type=F32)
        state[h] = chunk_dec[h] * s_prev + _dot_t((k * kd_ref[:, sl]).astype(BF16), vb)
        o_ref[0, :, sl] = _ret_finish(o, g_ref[0, :, sl], gr_ref[:, sl]).astype(o_ref.dtype)

    @pl.when(c == n_chunks - 1)
    def _():
        sfin_ref[0] = state[...]


def _ret_tables(cs, rows):
    lg = _ret_gamma_logs()
    i = np.arange(rows) % cs
    seq = np.arange(rows) // cs
    diff = i[:, None] - i[None, :]
    same = seq[:, None] == seq[None, :]
    dmask = np.where((diff >= 0)[None] & same[None], np.exp(diff[None] * lg[:, None, None]), 0.0)
    k_dec = np.exp((cs - 1 - i)[:, None] * lg[None, :])
    q_dec = np.exp((i + 1)[:, None] * lg[None, :])
    k_dec = np.repeat(k_dec, 128, axis=1)
    q_dec = np.repeat(q_dec, 128, axis=1)
    chunk_dec = tuple(float(x) for x in np.exp(cs * lg))
    return (jnp.asarray(dmask, F32), jnp.asarray(q_dec, F32), jnp.asarray(k_dec, F32), chunk_dec)


def _rope_tables(pos):
    half = DK_RET // 2
    inv = ROPE_BASE ** (-jnp.arange(half, dtype=F32) / half)
    ang = pos.astype(F32)[:, None] * inv[None, :]
    cos = jnp.cos(ang)
    sin = jnp.sin(ang)
    return jnp.concatenate([cos, cos], axis=-1), jnp.concatenate([-sin, sin], axis=-1)


def retention_prompt(y3, rope_cos, rope_sin, g_ret):
    b, l, _ = y3.shape
    cs = RET_CHUNK
    nc = l // cs
    dmask, q_dec, k_dec, chunk_dec = _ret_tables(cs, cs)

    def col(cb):
        return pl.BlockSpec((1, cs, 512), lambda bi, c: (bi, c, cb))

    const2 = lambda shape: pl.BlockSpec(shape, lambda bi, c: (0,) * len(shape))
    return pl.pallas_call(
        functools.partial(_ret_prompt_kernel, chunk_dec=chunk_dec, n_chunks=nc),
        grid=(b, nc),
        in_specs=[col(CB_RQ), col(CB_RK), col(CB_RV), col(CB_RG),
                  pl.BlockSpec((cs, 128), lambda bi, c: (c, 0)),
                  pl.BlockSpec((cs, 128), lambda bi, c: (c, 0)),
                  const2((H_RET, cs, cs)), const2((cs, 512)), const2((cs, 512)), const2((1, 512))],
        out_specs=[pl.BlockSpec((1, cs, 512), lambda bi, c: (bi, c, 0)),
                   pl.BlockSpec((1, H_RET, 128, 128), lambda bi, c: (bi, 0, 0, 0))],
        out_shape=[jax.ShapeDtypeStruct((b, l, 512), BF16),
                   jax.ShapeDtypeStruct((b, H_RET, 128, 128), F32)],
        scratch_shapes=[pltpu.VMEM((H_RET, 128, 128), F32)],
        compiler_params=_params(32, 2),
        name="retention_prompt",
    )(y3, y3, y3, y3, rope_cos, rope_sin, dmask, q_dec, k_dec, g_ret.reshape(1, 512))


def _ret_sample_kernel(q_ref, k_ref, v_ref, g_ref, cos_ref, sin_ref, dm_ref, qd_ref, kd_ref, gr_ref, s0_ref,
                       o_ref, sfin_ref, o_state, *, chunk_dec, n_seq, cs):
    b = pl.program_id(0)

    @pl.when(b == 0)
    def _():
        o_state[...] = jnp.zeros_like(o_state)

    cos = cos_ref[...]
    sin = sin_ref[...]
    rows = q_ref.shape[1]
    in_seq = lax.shift_right_logical(lax.broadcasted_iota(jnp.int32, (rows, 128), 0), cs.bit_length() - 1) == b
    for h in range(H_RET):
        sl = slice(h * 128, (h + 1) * 128)
        q, k, v = _ret_head_qkv(q_ref, k_ref, v_ref, cos, sin, h)
        s_prev = s0_ref[0, 0, h]
        qm = jnp.where(in_seq, q * qd_ref[:, sl], 0.0)
        km = jnp.where(in_seq, k * kd_ref[:, sl], 0.0)
        o_state[:, sl] += jnp.dot(qm.astype(BF16), s_prev.astype(BF16), preferred_element_type=F32)
        sfin_ref[0, h] = chunk_dec[h] * s_prev + _dot_t(km.astype(BF16), v.astype(BF16))

    @pl.when(b == n_seq - 1)
    def _():
        for h in range(H_RET):
            sl = slice(h * 128, (h + 1) * 128)
            q, k, v = _ret_head_qkv(q_ref, k_ref, v_ref, cos, sin, h)
            scores = _dot_nt(q.astype(BF16), k.astype(BF16)) * dm_ref[h]
            o = jnp.dot(scores.astype(BF16), v.astype(BF16), preferred_element_type=F32) + o_state[:, sl]
            o_ref[0, :, sl] = _ret_finish(o, g_ref[0, :, sl], gr_ref[:, sl]).astype(o_ref.dtype)


def retention_sample(y3, rope_cos, rope_sin, g_ret, state_ret, layer, cs):
    _, t, _ = y3.shape
    n_seq = t // cs
    dmask, q_dec, k_dec, chunk_dec = _ret_tables(cs, t)

    def col(cb):
        return pl.BlockSpec((1, t, 512), lambda bi: (0, 0, cb))

    const = lambda shape: pl.BlockSpec(shape, lambda bi: (0,) * len(shape))
    return pl.pallas_call(
        functools.partial(_ret_sample_kernel, chunk_dec=chunk_dec, n_seq=n_seq, cs=cs),
        grid=(n_seq,),
        in_specs=[col(CB_RQ), col(CB_RK), col(CB_RV), col(CB_RG),
                  const((t, 128)), const((t, 128)),
                  const((H_RET, t, t)), const((t, 512)), const((t, 512)), const((1, 512)),
                  pl.BlockSpec((1, 1, H_RET, 128, 128), lambda bi: (bi, layer, 0, 0, 0))],
        out_specs=[pl.BlockSpec((1, t, 512), lambda bi: (0, 0, 0)),
                   pl.BlockSpec((1, H_RET, 128, 128), lambda bi: (bi, 0, 0, 0))],
        out_shape=[jax.ShapeDtypeStruct((1, t, 512), BF16),
                   jax.ShapeDtypeStruct((n_seq, H_RET, 128, 128), F32)],
        scratch_shapes=[pltpu.VMEM((t, 512), F32)],
        compiler_params=_params(32, 1),
        name="retention_sample",
    )(y3, y3, y3, y3, rope_cos, rope_sin, dmask, q_dec, k_dec, g_ret.reshape(1, 512), state_ret)


def _sgu_kernel(u_ref, v_ref, g_ref, w_ref, b_ref, o_ref, *rest, emit_vn):
    v = v_ref[0]
    ms = jnp.mean(v * v, axis=-1, keepdims=True)
    vn = v * lax.rsqrt(ms + EPS) * g_ref[...]
    if emit_vn:
        rest[0][0] = vn
    vb = vn.astype(BF16)
    for g in range(G_SGU):
        sl = slice(g * 128, (g + 1) * 128)
        mixed = jnp.dot(w_ref[g], vb[:, sl], preferred_element_type=F32) + b_ref[g]
        o_ref[0, :, sl] = (u_ref[0, :, sl] * mixed).astype(o_ref.dtype)


def spatial_gating(y3, g_sgu, w_mix, b_mix, emit_vn):
    b, l, _ = y3.shape
    rows = SGU_CHUNK
    nc = l // rows

    def col(cb):
        return pl.BlockSpec((1, rows, 512), lambda bi, c: (bi, c, cb))

    const = lambda shape: pl.BlockSpec(shape, lambda bi, c: (0,) * len(shape))
    out_spec = pl.BlockSpec((1, rows, 512), lambda bi, c: (bi, c, 0))
    out_shape = [jax.ShapeDtypeStruct((b, l, 512), BF16)]
    out_specs = [out_spec]
    if emit_vn:
        out_shape.append(jax.ShapeDtypeStruct((b, l, 512), F32))
        out_specs.append(out_spec)
    res = pl.pallas_call(
        functools.partial(_sgu_kernel, emit_vn=emit_vn),
        grid=(b, nc),
        in_specs=[col(CB_SU), col(CB_SV), const((1, 512)),
                  const((G_SGU, rows, rows)), const((G_SGU, rows, 128))],
        out_specs=out_specs, out_shape=out_shape,
        compiler_params=_params(32, 2),
        name="spatial_gating",
    )(y3, y3, g_sgu.reshape(1, 512), w_mix, b_mix)
    return res if emit_vn else (res[0], None)


def _qk_prep_kernel(q_ref, k_ref, v_ref, gq_ref, gk_ref, qn_ref, kn_ref, knb_ref, vb_ref):
    for h in range(H_SB):
        sl = slice(h * 128, (h + 1) * 128)
        q = q_ref[:, sl]
        k = k_ref[:, sl]
        qn = q * lax.rsqrt(jnp.mean(q * q, axis=-1, keepdims=True) + EPS) * gq_ref[...]
        kn = k * lax.rsqrt(jnp.mean(k * k, axis=-1, keepdims=True) + EPS) * gk_ref[...]
        qn_ref[:, sl] = (qn * Q_LOGIT_SCALE).astype(BF16)
        kn_ref[:, sl] = kn
        knb_ref[:, sl] = kn.astype(BF16)
    vb_ref[...] = v_ref[...].astype(BF16)


def qk_prep(y, g_q, g_k, tm):
    t = y.shape[0]

    def col(cb):
        return pl.BlockSpec((tm, 512), lambda i: (i, cb))

    out_spec = pl.BlockSpec((tm, 512), lambda i: (i, 0))
    return pl.pallas_call(
        _qk_prep_kernel,
        grid=(t // tm,),
        in_specs=[col(CB_BQ), col(CB_BK), col(CB_BV),
                  pl.BlockSpec((1, 128), lambda i: (0, 0)), pl.BlockSpec((1, 128), lambda i: (0, 0))],
        out_specs=[out_spec] * 4,
        out_shape=[jax.ShapeDtypeStruct((t, 512), BF16), jax.ShapeDtypeStruct((t, 512), F32),
                   jax.ShapeDtypeStruct((t, 512), BF16), jax.ShapeDtypeStruct((t, 512), BF16)],
        compiler_params=_params(32, 1),
        name="qk_prep",
    )(y, y, y, g_q.reshape(1, 128), g_k.reshape(1, 128))


def _cumsum_weights():
    j = np.arange(128)
    later = (j[:, None] > j[None, :]).astype(np.float32)
    w = np.concatenate([later, np.ones((128, 128), np.float32)], axis=1)
    return jnp.asarray(np.concatenate([w, w], axis=0), BF16)


def _sb_tile(z2, carry, uo, mask):
    base, r = _sb_stage1(z2, uo, mask)
    return _sb_stage2(base, r, carry)


MASKED_LOG = -1e30


def _sb_stage1(z2, uo, mask):
    lg2 = jnp.log(1.0 + jnp.exp2(-jnp.abs(z2))) * LOG2E
    log_1m = -(jnp.maximum(z2, 0.0) + lg2)
    base = z2 + log_1m
    if mask is not None:
        log_1m = jnp.where(mask, log_1m, 0.0)
        base = jnp.where(mask, base, MASKED_LOG)
    hi = log_1m.astype(BF16)
    lo = (log_1m - hi.astype(F32)).astype(BF16)
    r = jnp.dot(jnp.concatenate([hi, lo], axis=1), uo, preferred_element_type=F32)
    return base, r


def _sb_stage2(base, r, carry):
    a = jnp.exp2(base + (r[:, :128] + carry))
    return a, carry + r[:, 128:]


def _attn_prompt_kernel(bias_ref, q_ref, k_ref, v_ref, uo_ref, o_ref, *, tq):
    h = pl.program_id(1)
    i = pl.program_id(2)
    q = q_ref[0]
    bias2 = bias_ref[h] * LOG2E
    uo = uo_ref[...]
    n_band = tq // 128
    row = lax.broadcasted_iota(jnp.int32, (tq, 128), 0)
    colk = lax.broadcasted_iota(jnp.int32, (tq, 128), 1)

    def tile_start(t):
        return pl.multiple_of(jnp.maximum((i + 1) * n_band - 1 - t, 0) * 128, 128)

    def logits(t):
        return _dot_nt(q, k_ref[0, pl.ds(tile_start(t), 128), :]) + bias2

    def finish(t_prev, acc, log_a_prev):
        v = v_ref[0, pl.ds(tile_start(t_prev), 128), :]
        return acc + jnp.dot(jnp.exp2(log_a_prev).astype(BF16), v, preferred_element_type=F32)

    def trip(t, c, mask):
        carry, acc, z2, log_a_prev = c
        z_next = logits(t + 1)
        acc = finish(jnp.maximum(t - 1, 0), acc, log_a_prev)
        base, r = _sb_stage1(z2, uo, mask)
        return carry + r[:, 128:], acc, z_next, base + (r[:, :128] + carry)

    zero = jnp.zeros((tq, 128), F32)
    c = (zero, zero, logits(0), jnp.full((tq, 128), MASKED_LOG, F32))
    for t in range(n_band):
        c = trip(t, c, colk + (n_band - 1 - t) * 128 < row)
    c = lax.fori_loop(n_band, (i + 1) * n_band, lambda t, c: trip(t, c, None), c)
    acc = finish((i + 1) * n_band - 1, c[1], c[3])
    o_ref[0] = acc.astype(o_ref.dtype)


def attention_prompt(qn, knb, vb, b_sb, uo, tq=512):
    b, l, _ = qn.shape
    tq = min(tq, l)
    return pl.pallas_call(
        functools.partial(_attn_prompt_kernel, tq=tq),
        grid=(b, H_SB, l // tq),
        in_specs=[pl.BlockSpec(memory_space=pltpu.SMEM),
                  pl.BlockSpec((1, tq, 128), lambda bi, h, i: (bi, i, h)),
                  pl.BlockSpec((1, l, 128), lambda bi, h, i: (bi, 0, h)),
                  pl.BlockSpec((1, l, 128), lambda bi, h, i: (bi, 0, h)),
                  pl.BlockSpec((256, 256), lambda bi, h, i: (0, 0))],
        out_specs=pl.BlockSpec((1, tq, 128), lambda bi, h, i: (bi, i, h)),
        out_shape=jax.ShapeDtypeStruct((b, l, 512), BF16),
        compiler_params=_params(32, 3),
        name="attention_prompt",
    )(b_sb, qn, knb, vb, uo)


ROWS_PER_HEAD = 8


def _attn_sample_kernel(pt_ref, q_ref, kn_ref, vn_ref, brow_ref, uo_ref, *rest, n_slots, n_steps):
    kp = rest[:n_slots]
    vp = rest[n_slots:2 * n_slots]
    o_ref, carry_s, acc_s = rest[2 * n_slots:]
    j = pl.program_id(1)
    uo = uo_ref[...]
    brow = brow_ref[...]
    heads = range(H_SB)
    rows = [slice(h * ROWS_PER_HEAD, (h + 1) * ROWS_PER_HEAD) for h in heads]
    qs = [q_ref[0, rows[h], :] for h in heads]

    def logits(k_of_head):
        return jnp.concatenate([_dot_nt(qs[h], k_of_head(h)) for h in heads], axis=0) + brow

    def accumulate(a, v_of_head):
        a = a.astype(BF16)
        for h in heads:
            acc_s[rows[h], :] += jnp.dot(a[rows[h]], v_of_head(h), preferred_element_type=F32)

    @pl.when(j == 0)
    def _():
        acc_s[...] = jnp.zeros_like(acc_s)
        m = brow.shape[0]
        tok = lax.broadcasted_iota(jnp.int32, (m, 128), 0) & (ROWS_PER_HEAD - 1)
        key = lax.broadcasted_iota(jnp.int32, (m, 128), 1)
        a, carry = _sb_tile(logits(lambda h: kn_ref[0, h]), jnp.zeros((m, 128), F32), uo, key < tok)
        carry_s[...] = carry
        accumulate(a, lambda h: vn_ref[0, h])

    def page_head(ref, h):
        return ref[0, 0, pl.ds(h, 128, stride=H_SB), :].astype(BF16)

    stage1 = [_sb_stage1(logits(lambda h, s=s: page_head(kp[s], h)), uo, None) for s in range(n_slots)]
    carry = carry_s[...]
    for s in range(n_slots):
        a, carry = _sb_stage2(*stage1[s], carry)
        accumulate(a, lambda h, s=s: page_head(vp[s], h))
    carry_s[...] = carry

    @pl.when(j == n_steps - 1)
    def _():
        o_ref[0] = acc_s[...]


def attention_sample(q8, k_new, v_new, brow, uo, cache_k, cache_v, page_table, layer, n_slots=4):
    n_seq, n_pages = page_table.shape
    n_steps = n_pages // n_slots
    m = q8.shape[1]
    page_rows = cache_k.shape[2]

    def page_spec(s):
        return pl.BlockSpec((1, 1, page_rows, 128),
                            lambda bi, j, pt: (pt[bi, n_pages - 1 - (j * n_slots + s)], layer, 0, 0))

    per_seq = lambda shape: pl.BlockSpec(shape, lambda bi, j, pt: (bi,) + (0,) * (len(shape) - 1))
    const = lambda shape: pl.BlockSpec(shape, lambda bi, j, pt: (0,) * len(shape))
    grid_spec = pltpu.PrefetchScalarGridSpec(
        num_scalar_prefetch=1,
        grid=(n_seq, n_steps),
        in_specs=[per_seq((1, m, 128)), per_seq((1, H_SB, 128, 128)), per_seq((1, H_SB, 128, 128)),
                  const((m, 128)), const((256, 256))]
                 + [page_spec(s) for s in range(n_slots)] * 2,
        out_specs=per_seq((1, m, 128)),
        scratch_shapes=[pltpu.VMEM((m, 128), F32), pltpu.VMEM((m, 128), F32)],
    )
    return pl.pallas_call(
        functools.partial(_attn_sample_kernel, n_slots=n_slots, n_steps=n_steps),
        grid_spec=grid_spec,
        out_shape=jax.ShapeDtypeStruct((n_seq, m, 128), F32),
        compiler_params=_params(32, 2),
        name="attention_sample",
    )(page_table, q8, k_new, v_new, brow, uo, *([cache_k] * n_slots), *([cache_v] * n_slots))


def _merge_kernel(x_ref, oa_ref, ob_ref, oc_ref, gates_ref, wb_ref, wo_ref, ga_ref, o_ref):
    merged = None
    for br, o_b in enumerate((oa_ref, ob_ref, oc_ref)):
        gate = jax.nn.sigmoid(gates_ref[:, br * D_MODEL:(br + 1) * D_MODEL])
        term = gate * jnp.dot(o_b[...], wb_ref[br], preferred_element_type=F32)
        merged = term if merged is None else merged + term
    y = jnp.dot(merged.astype(BF16), wo_ref[...], preferred_element_type=F32)
    o_ref[...] = x_ref[...] + ga_ref[0] * y


def merge_out(x, o_ret, o_sgu, o_sb, y, w_branch, w_out, ga, *, tm, rows_per_mod):
    t, d = x.shape
    r = ga.shape[1]
    tiles_per_mod = rows_per_mod // tm
    row = lambda w: pl.BlockSpec((tm, w), lambda i: (i, 0))
    return pl.pallas_call(
        _merge_kernel,
        grid=(t // tm,),
        in_specs=[row(d), row(512), row(512), row(512), row(N_GATE_COLS),
                  pl.BlockSpec((3, BRANCH_W, d), lambda i: (0, 0, 0)),
                  pl.BlockSpec((d, d), lambda i: (0, 0)),
                  pl.BlockSpec((1, r, d), lambda i: (i // tiles_per_mod, 0, 0))],
        out_specs=row(d),
        out_shape=jax.ShapeDtypeStruct((t, d), F32),
        compiler_params=_params(48, 1),
        name="merge_out",
    )(x, o_ret, o_sgu, o_sb, y, w_branch, w_out, ga)


def _topk_rows(vals, ids, k):
    rank = jnp.full(vals.shape, NOT_RANKED, F32)
    tops = []
    for a in range(k):
        m = jnp.max(vals, axis=0, keepdims=True)
        first = jnp.min(jnp.where(vals == m, ids, 1e9), axis=0, keepdims=True)
        hit = ids == first
        rank = jnp.where(hit, float(a), rank)
        vals = jnp.where(hit, -jnp.inf, vals)
        tops.append(m)
    return tops, rank, vals


def _peer_select_chunk(s1, s2):
    tt = s1.shape[1]
    key_id = lax.broadcasted_iota(jnp.int32, (PEER_NKEYS, tt), 0).astype(F32)
    top1, rank1, _ = _topk_rows(s1, key_id, PEER_TOPK)
    top2_rows, rank2, _ = _topk_rows(s2, key_id, PEER_TOPK)
    top_id = lax.broadcasted_iota(jnp.int32, (PEER_TOPK, tt), 0)
    t1 = jnp.zeros((PEER_TOPK, tt), F32)
    t2 = jnp.zeros((PEER_TOPK, tt), F32)
    for r in range(PEER_TOPK):
        t1 = jnp.where(top_id == r, top1[r], t1)
        t2 = jnp.where(top_id == r, top2_rows[r], t2)
    half = PEER_TOPK // 2
    rank_id = top_id.astype(F32)
    slabs = [t1 + top2_rows[0], top1[0] + t2[half:]]
    ids = [rank_id * PEER_TOPK, rank_id[half:]]
    for b in range(1, half):
        slabs.append(t1[:half] + top2_rows[b])
        ids.append(rank_id[:half] * PEER_TOPK + float(b))
    _, _, left = _topk_rows(jnp.concatenate(slabs, axis=0), jnp.concatenate(ids, axis=0), PEER_TOPK)
    chosen = jnp.where(left == -jnp.inf, 1.0, 0.0)
    ch_b0 = chosen[:PEER_TOPK]
    ch_a0 = chosen[PEER_TOPK:PEER_TOPK + half]
    e1_top = jnp.exp(t1 - top1[0])
    e2_top = jnp.exp(t2 - top2_rows[0])
    inner_cnt = jnp.zeros((half, tt), F32)
    inner_e2 = jnp.zeros((half, tt), F32)
    for b in range(1, half):
        ch = chosen[PEER_TOPK + b * half:PEER_TOPK + (b + 1) * half]
        inner_cnt = inner_cnt + ch
        inner_e2 = inner_e2 + ch * e2_top[b:b + 1]
    tail_cnt = jnp.sum(ch_a0, axis=0, keepdims=True)
    z = (jnp.sum(ch_b0 * e1_top, axis=0, keepdims=True) + jnp.sum(ch_a0 * e2_top[half:], axis=0, keepdims=True)
         + jnp.sum(inner_e2 * e1_top[:half], axis=0, keepdims=True))
    counts = [ch_b0[a:a + 1] + (inner_cnt[a:a + 1] if a < half else 0.0) + (tail_cnt if a == 0 else 0.0)
              for a in range(PEER_TOPK)]
    c1 = jnp.zeros((PEER_NKEYS, tt), F32)
    for a in range(PEER_TOPK):
        c1 = jnp.where(rank1 == float(a), counts[a], c1)
    f1 = jnp.exp(s1 - top1[0]) / z
    return c1, f1, rank2.astype(BF16), jnp.exp(s2 - top2_rows[0]).astype(BF16)


def _peer_select_kernel(q_ref, sk_ref, c1_ref, f1_ref, r2_ref, f2_ref):
    tt = q_ref.shape[0]
    grouped = (PEER_NKEYS // BF16_ROWS, BF16_ROWS, LANES)
    for c in range(tt // LANES):
        tok = slice(c * LANES, (c + 1) * LANES)
        s1, s2 = [_dot_nt(sk_ref[p], q_ref[tok, p * 128:(p + 1) * 128], precision=lax.Precision.HIGHEST)
                  for p in range(2)]
        c1, f1, r2, f2 = _peer_select_chunk(s1, s2)
        c1_ref[0, :, tok] = c1
        f1_ref[0, :, tok] = f1
        r2_ref[0, :, :, tok] = r2.reshape(grouped)
        f2_ref[0, :, :, tok] = f2.reshape(grouped)


def peer_select(q, sub_keys, tt):
    t = q.shape[0]
    n_groups = PEER_NKEYS // BF16_ROWS
    shape1 = jax.ShapeDtypeStruct((PEER_HEADS, PEER_NKEYS, t), F32)
    shape2 = jax.ShapeDtypeStruct((PEER_HEADS, n_groups, BF16_ROWS, t), BF16)
    spec1 = pl.BlockSpec((1, PEER_NKEYS, tt), lambda i, h: (h, 0, i))
    spec2 = pl.BlockSpec((1, n_groups, BF16_ROWS, tt), lambda i, h: (h, 0, 0, i))
    return pl.pallas_call(
        _peer_select_kernel,
        grid=(t // tt, PEER_HEADS),
        in_specs=[pl.BlockSpec((tt, 256), lambda i, h: (i, h)),
                  pl.BlockSpec((2, PEER_NKEYS, 128), lambda i, h: (0, 0, 0))],
        out_specs=[spec1, spec1, spec2, spec2], out_shape=[shape1, shape1, shape2, shape2],
        compiler_params=_params(32, 2),
        name="peer_select",
    )(q, sub_keys)


def _gelu_tanh(x):
    k = -2.0 * 0.7978845608028654 * LOG2E
    return x / (1.0 + jnp.exp2(x * (k + (k * 0.044715) * (x * x))))


def _peer_dense_kernel(h_ref, u_ref, vt_ref, c1_ref, f1_ref, r2_ref, f2_ref, x_ref, ga_ref, o_ref,
                       acc, act, wact, *, n_rows, n_steps):
    e = pl.program_id(1)
    tt = h_ref.shape[0]

    @pl.when(e == 0)
    def _():
        acc[...] = jnp.zeros_like(acc)

    chunk = min(tt, 2 * LANES)
    zero = jnp.zeros((BF16_ROWS, chunk), BF16)
    n_groups = PEER_NKEYS // BF16_ROWS
    te = u_ref.shape[0]
    rows_per_part = n_rows // N_DENSE_PARTS
    tp = rows_per_part * PEER_NKEYS
    gp = rows_per_part * n_groups

    def project(p):
        return _dot_nt(u_ref[p * tp:(p + 1) * tp, :], h_ref[...])

    def combine(p):
        return jnp.dot(vt_ref[:, p * tp:(p + 1) * tp], wact[p * gp:(p + 1) * gp].reshape(tp, tt),
                       preferred_element_type=F32)

    raw_next = project(0)
    for p in range(N_DENSE_PARTS):
        raw = raw_next
        if p + 1 < N_DENSE_PARTS:
            raw_next = project(p + 1)
        act[p * gp:(p + 1) * gp] = _gelu_tanh(raw).astype(BF16).reshape(gp, BF16_ROWS, tt)
        if p > 0:
            acc[...] += combine(p - 1)
        for il in range(p * rows_per_part, (p + 1) * rows_per_part):
            for tc in range(tt // chunk):
                lanes = slice(tc * chunk, (tc + 1) * chunk)
                w = [zero] * n_groups
                for h in range(PEER_HEADS):
                    c1 = jnp.broadcast_to(c1_ref[h, il:il + 1, lanes], (BF16_ROWS, chunk)).astype(BF16)
                    f1 = jnp.broadcast_to(f1_ref[h, il:il + 1, lanes], (BF16_ROWS, chunk)).astype(BF16)
                    for g in range(n_groups):
                        w[g] = w[g] + jnp.where(r2_ref[h, g, :, lanes] < c1, f2_ref[h, g, :, lanes], zero) * f1
                for g in range(n_groups):
                    wact[il * n_groups + g, :, lanes] = w[g] * act[il * n_groups + g, :, lanes]
    acc[...] += combine(N_DENSE_PARTS - 1)

    @pl.when(e == n_steps - 1)
    def _():
        o_ref[...] = x_ref[...] + ga_ref[0] * acc[...].T


def peer_dense(x, h2, exp_u, exp_vt, sel, ga, *, tt, rows_per_mod, n_rows=8):
    t, d = x.shape
    n_exp = exp_u.shape[0]
    te = n_rows * PEER_NKEYS
    n_steps = n_exp // te
    r = ga.shape[1]
    tiles_per_mod = rows_per_mod // tt
    row_spec = pl.BlockSpec((PEER_HEADS, n_rows, tt), lambda i, e: (0, e, i))
    n_groups = PEER_NKEYS // BF16_ROWS
    full_spec = pl.BlockSpec((PEER_HEADS, n_groups, BF16_ROWS, tt), lambda i, e: (0, 0, 0, i))
    grouped = pltpu.VMEM((te // BF16_ROWS, BF16_ROWS, tt), BF16)
    return pl.pallas_call(
        functools.partial(_peer_dense_kernel, n_rows=n_rows, n_steps=n_steps),
        grid=(t // tt, n_steps),
        in_specs=[pl.BlockSpec((tt, d), lambda i, e: (i, 0)),
                  pl.BlockSpec((te, d), lambda i, e: (e, 0)),
                  pl.BlockSpec((d, te), lambda i, e: (0, e)),
                  row_spec, row_spec, full_spec, full_spec,
                  pl.BlockSpec((tt, d), lambda i, e: (i, 0)),
                  pl.BlockSpec((1, r, d), lambda i, e: (i // tiles_per_mod, 0, 0))],
        out_specs=pl.BlockSpec((tt, d), lambda i, e: (i, 0)),
        out_shape=jax.ShapeDtypeStruct((t, d), F32),
        scratch_shapes=[pltpu.VMEM((d, tt), F32), grouped, grouped],
        compiler_params=_params(56, 2),
        name="peer_dense",
    )(h2, exp_u, exp_vt, *sel, x, ga)


def _token_tile(rows_per_mod):
    return 512 if rows_per_mod % 512 == 0 else rows_per_mod


def _layer(x, mods, lw, *, n_batch, sample):
    t, d = x.shape
    l = t // n_batch
    sh1, sc1, ga1, sh2, sc2, ga2 = mods
    rows_per_mod = l if sample is None else t
    tm = _token_tile(rows_per_mod)
    tm_in = 2 * tm if rows_per_mod % (2 * tm) == 0 else tm
    y = norm_mod_matmul(x, lw["g_norm1"], sc1, sh1, lw["w_in"], tm=tm_in, tn=1536, rows_per_mod=rows_per_mod)

    qn, kn, knb, vb = qk_prep(y, lw["g_q"], lw["g_k"], tm)
    if sample is None:
        y3 = y.reshape(n_batch, l, -1)
        o_ret, s_new = retention_prompt(y3, lw["rope_cos"], lw["rope_sin"], lw["g_ret"])
        o_sgu, vn_rows = spatial_gating(y3, lw["g_sgu"], lw["w_mix"], lw["b_mix"], emit_vn=False)
        o_sb = attention_prompt(qn.reshape(n_batch, l, 512), knb.reshape(n_batch, l, 512),
                                vb.reshape(n_batch, l, 512), lw["b_sb"], lw["uo"])
    else:
        y3 = y.reshape(1, t, -1)
        o_ret, s_new = retention_sample(y3, lw["rope_cos"], lw["rope_sin"], lw["g_ret"],
                                        sample["state_ret"], sample["layer"], l)
        o_sgu, vn_rows = spatial_gating(y3, lw["g_sgu"], lw["w_mix"], lw["b_mix"], emit_vn=True)
        def head_major(a, rows):
            a = jnp.transpose(a.reshape(n_batch, l, H_SB, D_SB), (0, 2, 1, 3))
            return jnp.pad(a, ((0, 0), (0, 0), (0, rows - l), (0, 0)))

        q8 = head_major(qn, ROWS_PER_HEAD).reshape(n_batch, H_SB * ROWS_PER_HEAD, D_SB)
        brow = jnp.broadcast_to(jnp.repeat(lw["b_sb"] * LOG2E, ROWS_PER_HEAD)[:, None],
                                (H_SB * ROWS_PER_HEAD, 128))
        o8 = attention_sample(q8, head_major(knb, 128), head_major(vb, 128), brow, lw["uo"],
                              sample["cache_k"], sample["cache_v"], sample["page_table"], sample["layer"])
        o8 = o8.reshape(n_batch, H_SB, ROWS_PER_HEAD, D_SB)[:, :, :l]
        o_sb = jnp.transpose(o8, (0, 2, 1, 3)).reshape(t, 512).astype(BF16)

    x = merge_out(x, o_ret.reshape(t, 512), o_sgu.reshape(t, 512), o_sb.reshape(t, 512), y,
                  lw["w_branch"], lw["w_out"], ga1, tm=tm, rows_per_mod=rows_per_mod)

    pq, h2 = norm_mod_matmul(x, lw["g_norm2"], sc2, sh2, lw["w_pq"], tm=tm, tn=512,
                             rows_per_mod=rows_per_mod, emit_h=True, precise=True)
    tt = tm
    sel = peer_select(pq, lw["sub_keys"], tt)
    x = peer_dense(x, h2, lw["exp_u"], lw["exp_vt"], sel, ga2, tt=tt, rows_per_mod=rows_per_mod)
    v_raw = y[:, CB_BV * 512:(CB_BV + 1) * 512]
    return x, s_new, kn, v_raw, vn_rows


def kernel(x_prompt, x_sample, cache_k, cache_v, state_ret, page_table, c_prompt, c_sample, w_ada, b_ada,
           g_norm1, g_norm2, w_in, g_ret, g_q, g_k, b_sb, g_sgu, w_sp, b_sp, w_branch, w_out, w_pq,
           sub_keys, exp_u, exp_v):
    n_b, seq, d = x_prompt.shape
    n_s, dec, _ = x_sample.shape
    depth = w_in.shape[0]
    n_pool, _, page, _, _ = cache_k.shape
    past_len = page_table.shape[1] * page
    t_s = n_s * dec

    mod = ada_mod(jnp.concatenate([c_prompt, c_sample], axis=0), w_ada, b_ada)
    cache_k4 = cache_k.reshape(n_pool, depth, page * H_SB, D_SB)
    cache_v4 = cache_v.reshape(n_pool, depth, page * H_SB, D_SB)
    uo = _cumsum_weights()
    cos_p, sin_p = _rope_tables(jnp.arange(seq))
    cos_s, sin_s = _rope_tables(past_len + (jnp.arange(t_s) % dec))

    pos = np.arange(t_s)
    same_seq = jnp.asarray((pos[:, None] // dec) == (pos[None, :] // dec))
    tri = jnp.asarray(np.tril(np.ones((SGU_CHUNK, SGU_CHUNK), bool)))

    xp = x_prompt.reshape(n_b * seq, d)
    xs = x_sample.reshape(t_s, d)
    outs = {k: [] for k in ("kp", "vp", "ks", "vs", "rp", "rs", "us")}
    for l in range(depth):
        w_in_l = jnp.concatenate([w_in[l][:, -N_GATE_COLS:], w_in[l][:, :-N_GATE_COLS]], axis=1).astype(BF16)
        shared = dict(g_norm1=g_norm1[l], g_norm2=g_norm2[l], w_in=w_in_l, g_ret=g_ret[l], g_q=g_q[l], g_k=g_k[l],
                      b_sb=b_sb[l], g_sgu=g_sgu[l], w_branch=w_branch[l].astype(BF16), w_out=w_out[l].astype(BF16),
                      w_pq=_split_bf16(w_pq[l]), sub_keys=sub_keys[l], exp_u=exp_u[l].astype(BF16),
                      exp_vt=exp_v[l].T.astype(BF16), uo=uo)
        mods_p = [m.reshape(n_b, 1, d) for m in jnp.split(mod[l, :n_b], 6, axis=-1)]
        mods_s = [jnp.repeat(m, dec, axis=0).reshape(1, t_s, d) for m in jnp.split(mod[l, n_b:], 6, axis=-1)]

        w_tri = jnp.where(tri[None], w_sp[l], 0.0)
        lw_p = dict(shared, rope_cos=cos_p, rope_sin=sin_p, w_mix=w_tri.astype(BF16),
                    b_mix=jnp.broadcast_to(b_sp[l][:, :, None], (G_SGU, SGU_CHUNK, 128)))
        xp, rp, kp, vp, _ = _layer(xp, mods_p, lw_p, n_batch=n_b, sample=None)

        w_blk = jnp.where(same_seq[None], jnp.tile(w_tri[:, :dec, :dec], (1, n_s, n_s)), 0.0)
        b_blk = jnp.tile(b_sp[l][:, :dec], (1, n_s))
        lw_s = dict(shared, rope_cos=cos_s, rope_sin=sin_s, w_mix=w_blk.astype(BF16),
                    b_mix=jnp.broadcast_to(b_blk[:, :, None], (G_SGU, t_s, 128)))
        sample = dict(state_ret=state_ret, cache_k=cache_k4, cache_v=cache_v4, page_table=page_table, layer=l)
        xs, rs, ks, vs, us = _layer(xs, mods_s, lw_s, n_batch=n_s, sample=sample)

        outs["kp"].append(kp.reshape(n_b, seq, H_SB, D_SB))
        outs["vp"].append(vp.reshape(n_b, seq, H_SB, D_SB))
        outs["ks"].append(ks.reshape(n_s, dec, H_SB, D_SB))
        outs["vs"].append(vs.reshape(n_s, dec, H_SB, D_SB))
        outs["rp"].append(rp)
        outs["rs"].append(rs)
        outs["us"].append(us.reshape(n_s, dec, -1))
    stack = lambda name: jnp.stack(outs[name], axis=1)
    return (xp.reshape(n_b, seq, d), xs.reshape(n_s, dec, d),
            stack("kp"), stack("vp"), stack("ks"), stack("vs"), stack("rp"), stack("rs"), stack("us"))
```

```python
import functools

import numpy as np
import jax
import jax.numpy as jnp
from jax import lax
from jax.experimental import pallas as pl
from jax.experimental.pallas import tpu as pltpu

F32 = jnp.float32
BF16 = jnp.bfloat16

D_MODEL = 1024
H_RET = 4
DK_RET = 128
RET_CHUNK = 128
ROPE_BASE = 10000.0
G_SGU = 4
SGU_CHUNK = 128
H_SB = 4
D_SB = 128
BRANCH_W = 512
PEER_HEADS = 8
PEER_NKEYS = 128
PEER_TOPK = 16
EPS = 1e-6

LOG2E = 1.4426950408889634
Q_LOGIT_SCALE = D_SB ** -0.5 * LOG2E
LANES = 128
BF16_ROWS = 16
N_DENSE_PARTS = 4
MIB = 2 ** 20
NOT_RANKED = 99.0
N_GATE_COLS = 3 * D_MODEL
CB_RQ, CB_RK, CB_RV, CB_RG, CB_SU, CB_SV, CB_BQ, CB_BK, CB_BV = range(6, 15)


def _params(vmem_mib, n_grid):
    return pltpu.CompilerParams(dimension_semantics=("arbitrary",) * n_grid,
                                vmem_limit_bytes=vmem_mib * MIB)


def _ada_kernel(c_ref, w_ref, b_ref, o_ref):
    c = c_ref[...]
    s = (c * jax.nn.sigmoid(c)).astype(BF16)
    o_ref[0] = jnp.dot(s, w_ref[0].astype(BF16), preferred_element_type=F32) + b_ref[0]


def ada_mod(c_all, w_ada, b_ada):
    depth, d, n = w_ada.shape
    m = c_all.shape[0]
    tn = 1536
    return pl.pallas_call(
        _ada_kernel,
        grid=(depth, n // tn),
        in_specs=[pl.BlockSpec((m, d), lambda l, j: (0, 0)),
                  pl.BlockSpec((1, d, tn), lambda l, j: (l, 0, j)),
                  pl.BlockSpec((1, 1, tn), lambda l, j: (l, 0, j))],
        out_specs=pl.BlockSpec((1, m, tn), lambda l, j: (l, 0, j)),
        out_shape=jax.ShapeDtypeStruct((depth, m, n), F32),
        compiler_params=_params(40, 2),
        name="ada_mod",
    )(c_all, w_ada, b_ada.reshape(depth, 1, n))


def _nmm_kernel(x_ref, g_ref, sc_ref, sh_ref, *rest, emit_h, precise):
    n_w = 2 if precise else 1
    w_refs, y_ref, scratch = rest[:n_w], rest[n_w], rest[-n_w:]

    @pl.when(pl.program_id(1) == 0)
    def _():
        x = x_ref[...]
        ms = jnp.mean(x * x, axis=-1, keepdims=True)
        y = x * lax.rsqrt(ms + EPS) * g_ref[...]
        h = y * (1.0 + sc_ref[0]) + sh_ref[0]
        h_hi = h.astype(BF16)
        scratch[0][...] = h_hi
        if precise:
            scratch[1][...] = (h - h_hi.astype(F32)).astype(BF16)
        if emit_h:
            rest[n_w + 1][...] = h_hi

    dot = functools.partial(jnp.dot, preferred_element_type=F32)
    if precise:
        h_hi, h_lo = scratch[0][...], scratch[1][...]
        y_ref[...] = dot(h_hi, w_refs[0][...]) + (dot(h_hi, w_refs[1][...]) + dot(h_lo, w_refs[0][...]))
    else:
        y_ref[...] = dot(scratch[0][...], w_refs[0][...])


def _split_bf16(w):
    hi = w.astype(BF16)
    return hi, (w - hi.astype(F32)).astype(BF16)


def norm_mod_matmul(x, g, sc, sh, w, *, tm, tn, rows_per_mod, emit_h=False, precise=False):
    t, d = x.shape
    ws = tuple(w) if precise else (w,)
    n = ws[0].shape[1]
    r = sc.shape[1]
    tiles_per_mod = rows_per_mod // tm
    mod_spec = pl.BlockSpec((1, r, d), lambda i, j: (i // tiles_per_mod, 0, 0))
    out_shape = [jax.ShapeDtypeStruct((t, n), F32)]
    out_specs = [pl.BlockSpec((tm, tn), lambda i, j: (i, j))]
    if emit_h:
        out_shape.append(jax.ShapeDtypeStruct((t, d), BF16))
        out_specs.append(pl.BlockSpec((tm, d), lambda i, j: (i, 0)))
    res = pl.pallas_call(
        functools.partial(_nmm_kernel, emit_h=emit_h, precise=precise),
        grid=(t // tm, n // tn),
        in_specs=[pl.BlockSpec((tm, d), lambda i, j: (i, 0)),
                  pl.BlockSpec((1, d), lambda i, j: (0, 0)),
                  mod_spec, mod_spec] + [pl.BlockSpec((d, tn), lambda i, j: (0, j))] * len(ws),
        out_specs=out_specs,
        out_shape=out_shape,
        scratch_shapes=[pltpu.VMEM((tm, d), BF16)] * len(ws),
        compiler_params=_params(48, 2),
        name="norm_mod_matmul",
    )(x, g.reshape(1, d), sc, sh, *ws)
    return res if emit_h else res[0]


def _rope(x, cos_full, sin_signed):
    return x * cos_full + pltpu.roll(x, 64, 1) * sin_signed


def _ret_gamma_logs():
    return np.log(1.0 - np.exp2(-5.0 - np.arange(H_RET, dtype=np.float64)))


def _dot_t(a, b):
    return lax.dot_general(a, b, (((0,), (0,)), ((), ())), preferred_element_type=F32)


def _dot_nt(a, b, **kw):
    return lax.dot_general(a, b, (((1,), (1,)), ((), ())), preferred_element_type=F32, **kw)


def _ret_head_qkv(q_ref, k_ref, v_ref, cos, sin, h):
    sl = slice(h * 128, (h + 1) * 128)
    q = _rope(q_ref[0, :, sl], cos, sin)
    k = _rope(k_ref[0, :, sl], cos, sin) * (DK_RET ** -0.5)
    v = v_ref[0, :, sl]
    return q, k, v


def _ret_finish(o, gate, g_row):
    mu = jnp.mean(o, axis=-1, keepdims=True)
    xc = o - mu
    var = jnp.mean(xc * xc, axis=-1, keepdims=True)
    y = xc * lax.rsqrt(var + EPS) * g_row
    return gate * jax.nn.sigmoid(gate) * y


def _ret_prompt_kernel(q_ref, k_ref, v_ref, g_ref, cos_ref, sin_ref, dm_ref, qd_ref, kd_ref, gr_ref,
                       o_ref, sfin_ref, state, *, chunk_dec, n_chunks):
    c = pl.program_id(1)

    @pl.when(c == 0)
    def _():
        state[...] = jnp.zeros_like(state)

    cos = cos_ref[...]
    sin = sin_ref[...]
    for h in range(H_RET):
        sl = slice(h * 128, (h + 1) * 128)
        q, k, v = _ret_head_qkv(q_ref, k_ref, v_ref, cos, sin, h)
        vb = v.astype(BF16)
        scores = _dot_nt(q.astype(BF16), k.astype(BF16)) * dm_ref[h]
        s_prev = state[h]
        o = jnp.dot(scores.astype(BF16), vb, preferred_element_type=F32)
        o = o + jnp.dot((q * qd_ref[:, sl]).astype(BF16), s_prev.astype(BF16), preferred_element_type=F32)
        state[h] = chunk_dec[h] * s_prev + _dot_t((k * kd_ref[:, sl]).astype(BF16), vb)
        o_ref[0, :, sl] = _ret_finish(o, g_ref[0, :, sl], gr_ref[:, sl]).astype(o_ref.dtype)

    @pl.when(c == n_chunks - 1)
    def _():
        sfin_ref[0] = state[...]


def _ret_tables(cs, rows):
    lg = _ret_gamma_logs()
    i = np.arange(rows) % cs
    seq = np.arange(rows) // cs
    diff = i[:, None] - i[None, :]
    same = seq[:, None] == seq[None, :]
    dmask = np.where((diff >= 0)[None] & same[None], np.exp(diff[None] * lg[:, None, None]), 0.0)
    k_dec = np.exp((cs - 1 - i)[:, None] * lg[None, :])
    q_dec = np.exp((i + 1)[:, None] * lg[None, :])
    k_dec = np.repeat(k_dec, 128, axis=1)
    q_dec = np.repeat(q_dec, 128, axis=1)
    chunk_dec = tuple(float(x) for x in np.exp(cs * lg))
    return (jnp.asarray(dmask, F32), jnp.asarray(q_dec, F32), jnp.asarray(k_dec, F32), chunk_dec)


def _rope_tables(pos):
    half = DK_RET // 2
    inv = ROPE_BASE ** (-jnp.arange(half, dtype=F32) / half)
    ang = pos.astype(F32)[:, None] * inv[None, :]
    cos = jnp.cos(ang)
    sin = jnp.sin(ang)
    return jnp.concatenate([cos, cos], axis=-1), jnp.concatenate([-sin, sin], axis=-1)


def retention_prompt(y3, rope_cos, rope_sin, g_ret):
    b, l, _ = y3.shape
    cs = RET_CHUNK
    nc = l // cs
    dmask, q_dec, k_dec, chunk_dec = _ret_tables(cs, cs)

    def col(cb):
        return pl.BlockSpec((1, cs, 512), lambda bi, c: (bi, c, cb))

    const2 = lambda shape: pl.BlockSpec(shape, lambda bi, c: (0,) * len(shape))
    return pl.pallas_call(
        functools.partial(_ret_prompt_kernel, chunk_dec=chunk_dec, n_chunks=nc),
        grid=(b, nc),
        in_specs=[col(CB_RQ), col(CB_RK), col(CB_RV), col(CB_RG),
                  pl.BlockSpec((cs, 128), lambda bi, c: (c, 0)),
                  pl.BlockSpec((cs, 128), lambda bi, c: (c, 0)),
                  const2((H_RET, cs, cs)), const2((cs, 512)), const2((cs, 512)), const2((1, 512))],
        out_specs=[pl.BlockSpec((1, cs, 512), lambda bi, c: (bi, c, 0)),
                   pl.BlockSpec((1, H_RET, 128, 128), lambda bi, c: (bi, 0, 0, 0))],
        out_shape=[jax.ShapeDtypeStruct((b, l, 512), BF16),
                   jax.ShapeDtypeStruct((b, H_RET, 128, 128), F32)],
        scratch_shapes=[pltpu.VMEM((H_RET, 128, 128), F32)],
        compiler_params=_params(32, 2),
        name="retention_prompt",
    )(y3, y3, y3, y3, rope_cos, rope_sin, dmask, q_dec, k_dec, g_ret.reshape(1, 512))


def _ret_sample_kernel(q_ref, k_ref, v_ref, g_ref, cos_ref, sin_ref, dm_ref, qd_ref, kd_ref, gr_ref, s0_ref,
                       o_ref, sfin_ref, o_state, *, chunk_dec, n_seq, cs):
    b = pl.program_id(0)

    @pl.when(b == 0)
    def _():
        o_state[...] = jnp.zeros_like(o_state)

    cos = cos_ref[...]
    sin = sin_ref[...]
    rows = q_ref.shape[1]
    in_seq = lax.shift_right_logical(lax.broadcasted_iota(jnp.int32, (rows, 128), 0), cs.bit_length() - 1) == b
    for h in range(H_RET):
        sl = slice(h * 128, (h + 1) * 128)
        q, k, v = _ret_head_qkv(q_ref, k_ref, v_ref, cos, sin, h)
        s_prev = s0_ref[0, 0, h]
        qm = jnp.where(in_seq, q * qd_ref[:, sl], 0.0)
        km = jnp.where(in_seq, k * kd_ref[:, sl], 0.0)
        o_state[:, sl] += jnp.dot(qm.astype(BF16), s_prev.astype(BF16), preferred_element_type=F32)
        sfin_ref[0, h] = chunk_dec[h] * s_prev + _dot_t(km.astype(BF16), v.astype(BF16))

    @pl.when(b == n_seq - 1)
    def _():
        for h in range(H_RET):
            sl = slice(h * 128, (h + 1) * 128)
            q, k, v = _ret_head_qkv(q_ref, k_ref, v_ref, cos, sin, h)
            scores = _dot_nt(q.astype(BF16), k.astype(BF16)) * dm_ref[h]
            o = jnp.dot(scores.astype(BF16), v.astype(BF16), preferred_element_type=F32) + o_state[:, sl]
            o_ref[0, :, sl] = _ret_finish(o, g_ref[0, :, sl], gr_ref[:, sl]).astype(o_ref.dtype)


def retention_sample(y3, rope_cos, rope_sin, g_ret, state_ret, layer, cs):
    _, t, _ = y3.shape
    n_seq = t // cs
    dmask, q_dec, k_dec, chunk_dec = _ret_tables(cs, t)

    def col(cb):
        return pl.BlockSpec((1, t, 512), lambda bi: (0, 0, cb))

    const = lambda shape: pl.BlockSpec(shape, lambda bi: (0,) * len(shape))
    return pl.pallas_call(
        functools.partial(_ret_sample_kernel, chunk_dec=chunk_dec, n_seq=n_seq, cs=cs),
        grid=(n_seq,),
        in_specs=[col(CB_RQ), col(CB_RK), col(CB_RV), col(CB_RG),
                  const((t, 128)), const((t, 128)),
                  const((H_RET, t, t)), const((t, 512)), const((t, 512)), const((1, 512)),
                  pl.BlockSpec((1, 1, H_RET, 128, 128), lambda bi: (bi, layer, 0, 0, 0))],
        out_specs=[pl.BlockSpec((1, t, 512), lambda bi: (0, 0, 0)),
                   pl.BlockSpec((1, H_RET, 128, 128), lambda bi: (bi, 0, 0, 0))],
        out_shape=[jax.ShapeDtypeStruct((1, t, 512), BF16),
                   jax.ShapeDtypeStruct((n_seq, H_RET, 128, 128), F32)],
        scratch_shapes=[pltpu.VMEM((t, 512), F32)],
        compiler_params=_params(32, 1),
        name="retention_sample",
    )(y3, y3, y3, y3, rope_cos, rope_sin, dmask, q_dec, k_dec, g_ret.reshape(1, 512), state_ret)


def _sgu_kernel(u_ref, v_ref, g_ref, w_ref, b_ref, o_ref, *rest, emit_vn):
    v = v_ref[0]
    ms = jnp.mean(v * v, axis=-1, keepdims=True)
    vn = v * lax.rsqrt(ms + EPS) * g_ref[...]
    if emit_vn:
        rest[0][0] = vn
    vb = vn.astype(BF16)
    for g in range(G_SGU):
        sl = slice(g * 128, (g + 1) * 128)
        mixed = jnp.dot(w_ref[g], vb[:, sl], preferred_element_type=F32) + b_ref[g]
        o_ref[0, :, sl] = (u_ref[0, :, sl] * mixed).astype(o_ref.dtype)


def spatial_gating(y3, g_sgu, w_mix, b_mix, emit_vn):
    b, l, _ = y3.shape
    rows = SGU_CHUNK
    nc = l // rows

    def col(cb):
        return pl.BlockSpec((1, rows, 512), lambda bi, c: (bi, c, cb))

    const = lambda shape: pl.BlockSpec(shape, lambda bi, c: (0,) * len(shape))
    out_spec = pl.BlockSpec((1, rows, 512), lambda bi, c: (bi, c, 0))
    out_shape = [jax.ShapeDtypeStruct((b, l, 512), BF16)]
    out_specs = [out_spec]
    if emit_vn:
        out_shape.append(jax.ShapeDtypeStruct((b, l, 512), F32))
        out_specs.append(out_spec)
    res = pl.pallas_call(
        functools.partial(_sgu_kernel, emit_vn=emit_vn),
        grid=(b, nc),
        in_specs=[col(CB_SU), col(CB_SV), const((1, 512)),
                  const((G_SGU, rows, rows)), const((G_SGU, rows, 128))],
        out_specs=out_specs, out_shape=out_shape,
        compiler_params=_params(32, 2),
        name="spatial_gating",
    )(y3, y3, g_sgu.reshape(1, 512), w_mix, b_mix)
    return res if emit_vn else (res[0], None)


def _qk_prep_kernel(q_ref, k_ref, v_ref, gq_ref, gk_ref, qn_ref, kn_ref, knb_ref, vb_ref):
    for h in range(H_SB):
        sl = slice(h * 128, (h + 1) * 128)
        q = q_ref[:, sl]
        k = k_ref[:, sl]
        qn = q * lax.rsqrt(jnp.mean(q * q, axis=-1, keepdims=True) + EPS) * gq_ref[...]
        kn = k * lax.rsqrt(jnp.mean(k * k, axis=-1, keepdims=True) + EPS) * gk_ref[...]
        qn_ref[:, sl] = (qn * Q_LOGIT_SCALE).astype(BF16)
        kn_ref[:, sl] = kn
        knb_ref[:, sl] = kn.astype(BF16)
    vb_ref[...] = v_ref[...].astype(BF16)


def qk_prep(y, g_q, g_k, tm):
    t = y.shape[0]

    def col(cb):
        return pl.BlockSpec((tm, 512), lambda i: (i, cb))

    out_spec = pl.BlockSpec((tm, 512), lambda i: (i, 0))
    return pl.pallas_call(
        _qk_prep_kernel,
        grid=(t // tm,),
        in_specs=[col(CB_BQ), col(CB_BK), col(CB_BV),
                  pl.BlockSpec((1, 128), lambda i: (0, 0)), pl.BlockSpec((1, 128), lambda i: (0, 0))],
        out_specs=[out_spec] * 4,
        out_shape=[jax.ShapeDtypeStruct((t, 512), BF16), jax.ShapeDtypeStruct((t, 512), F32),
                   jax.ShapeDtypeStruct((t, 512), BF16), jax.ShapeDtypeStruct((t, 512), BF16)],
        compiler_params=_params(32, 1),
        name="qk_prep",
    )(y, y, y, g_q.reshape(1, 128), g_k.reshape(1, 128))


def _cumsum_weights():
    j = np.arange(128)
    later = (j[:, None] > j[None, :]).astype(np.float32)
    w = np.concatenate([later, np.ones((128, 128), np.float32)], axis=1)
    return jnp.asarray(np.concatenate([w, w], axis=0), BF16)


def _sb_tile(z2, carry, uo, mask):
    base, r = _sb_stage1(z2, uo, mask)
    return _sb_stage2(base, r, carry)


MASKED_LOG = -1e30


def _sb_stage1(z2, uo, mask):
    lg2 = jnp.log(1.0 + jnp.exp2(-jnp.abs(z2))) * LOG2E
    log_1m = -(jnp.maximum(z2, 0.0) + lg2)
    base = z2 + log_1m
    if mask is not None:
        log_1m = jnp.where(mask, log_1m, 0.0)
        base = jnp.where(mask, base, MASKED_LOG)
    hi = log_1m.astype(BF16)
    lo = (log_1m - hi.astype(F32)).astype(BF16)
    r = jnp.dot(jnp.concatenate([hi, lo], axis=1), uo, preferred_element_type=F32)
    return base, r


def _sb_stage2(base, r, carry):
    a = jnp.exp2(base + (r[:, :128] + carry))
    return a, carry + r[:, 128:]


def _attn_prompt_kernel(bias_ref, q_ref, k_ref, v_ref, uo_ref, o_ref, *, tq):
    h = pl.program_id(1)
    i = pl.program_id(2)
    q = q_ref[0]
    bias2 = bias_ref[h] * LOG2E
    uo = uo_ref[...]
    n_band = tq // 128
    row = lax.broadcasted_iota(jnp.int32, (tq, 128), 0)
    colk = lax.broadcasted_iota(jnp.int32, (tq, 128), 1)

    def tile_start(t):
        return pl.multiple_of(jnp.maximum((i + 1) * n_band - 1 - t, 0) * 128, 128)

    def logits(t):
        return _dot_nt(q, k_ref[0, pl.ds(tile_start(t), 128), :]) + bias2

    def finish(t_prev, acc, log_a_prev):
        v = v_ref[0, pl.ds(tile_start(t_prev), 128), :]
        return acc + jnp.dot(jnp.exp2(log_a_prev).astype(BF16), v, preferred_element_type=F32)

    def trip(t, c, mask):
        carry, acc, z2, log_a_prev = c
        z_next = logits(t + 1)
        acc = finish(jnp.maximum(t - 1, 0), acc, log_a_prev)
        base, r = _sb_stage1(z2, uo, mask)
        return carry + r[:, 128:], acc, z_next, base + (r[:, :128] + carry)

    zero = jnp.zeros((tq, 128), F32)
    c = (zero, zero, logits(0), jnp.full((tq, 128), MASKED_LOG, F32))
    for t in range(n_band):
        c = trip(t, c, colk + (n_band - 1 - t) * 128 < row)
    c = lax.fori_loop(n_band, (i + 1) * n_band, lambda t, c: trip(t, c, None), c)
    acc = finish((i + 1) * n_band - 1, c[1], c[3])
    o_ref[0] = acc.astype(o_ref.dtype)


def attention_prompt(qn, knb, vb, b_sb, uo, tq=512):
    b, l, _ = qn.shape
    tq = min(tq, l)
    return pl.pallas_call(
        functools.partial(_attn_prompt_kernel, tq=tq),
        grid=(b, H_SB, l // tq),
        in_specs=[pl.BlockSpec(memory_space=pltpu.SMEM),
                  pl.BlockSpec((1, tq, 128), lambda bi, h, i: (bi, i, h)),
                  pl.BlockSpec((1, l, 128), lambda bi, h, i: (bi, 0, h)),
                  pl.BlockSpec((1, l, 128), lambda bi, h, i: (bi, 0, h)),
                  pl.BlockSpec((256, 256), lambda bi, h, i: (0, 0))],
        out_specs=pl.BlockSpec((1, tq, 128), lambda bi, h, i: (bi, i, h)),
        out_shape=jax.ShapeDtypeStruct((b, l, 512), BF16),
        compiler_params=_params(32, 3),
        name="attention_prompt",
    )(b_sb, qn, knb, vb, uo)


ROWS_PER_HEAD = 8


def _attn_sample_kernel(pt_ref, q_ref, kn_ref, vn_ref, brow_ref, uo_ref, *rest, n_slots, n_steps):
    kp = rest[:n_slots]
    vp = rest[n_slots:2 * n_slots]
    o_ref, carry_s, acc_s = rest[2 * n_slots:]
    j = pl.program_id(1)
    uo = uo_ref[...]
    brow = brow_ref[...]
    heads = range(H_SB)
    rows = [slice(h * ROWS_PER_HEAD, (h + 1) * ROWS_PER_HEAD) for h in heads]
    qs = [q_ref[0, rows[h], :] for h in heads]

    def logits(k_of_head):
        return jnp.concatenate([_dot_nt(qs[h], k_of_head(h)) for h in heads], axis=0) + brow

    def accumulate(a, v_of_head):
        a = a.astype(BF16)
        for h in heads:
            acc_s[rows[h], :] += jnp.dot(a[rows[h]], v_of_head(h), preferred_element_type=F32)

    @pl.when(j == 0)
    def _():
        acc_s[...] = jnp.zeros_like(acc_s)
        m = brow.shape[0]
        tok = lax.broadcasted_iota(jnp.int32, (m, 128), 0) & (ROWS_PER_HEAD - 1)
        key = lax.broadcasted_iota(jnp.int32, (m, 128), 1)
        a, carry = _sb_tile(logits(lambda h: kn_ref[0, h]), jnp.zeros((m, 128), F32), uo, key < tok)
        carry_s[...] = carry
        accumulate(a, lambda h: vn_ref[0, h])

    def page_head(ref, h):
        return ref[0, 0, pl.ds(h, 128, stride=H_SB), :].astype(BF16)

    stage1 = [_sb_stage1(logits(lambda h, s=s: page_head(kp[s], h)), uo, None) for s in range(n_slots)]
    carry = carry_s[...]
    for s in range(n_slots):
        a, carry = _sb_stage2(*stage1[s], carry)
        accumulate(a, lambda h, s=s: page_head(vp[s], h))
    carry_s[...] = carry

    @pl.when(j == n_steps - 1)
    def _():
        o_ref[0] = acc_s[...]


def attention_sample(q8, k_new, v_new, brow, uo, cache_k, cache_v, page_table, layer, n_slots=4):
    n_seq, n_pages = page_table.shape
    n_steps = n_pages // n_slots
    m = q8.shape[1]
    page_rows = cache_k.shape[2]

    def page_spec(s):
        return pl.BlockSpec((1, 1, page_rows, 128),
                            lambda bi, j, pt: (pt[bi, n_pages - 1 - (j * n_slots + s)], layer, 0, 0))

    per_seq = lambda shape: pl.BlockSpec(shape, lambda bi, j, pt: (bi,) + (0,) * (len(shape) - 1))
    const = lambda shape: pl.BlockSpec(shape, lambda bi, j, pt: (0,) * len(shape))
    grid_spec = pltpu.PrefetchScalarGridSpec(
        num_scalar_prefetch=1,
        grid=(n_seq, n_steps),
        in_specs=[per_seq((1, m, 128)), per_seq((1, H_SB, 128, 128)), per_seq((1, H_SB, 128, 128)),
                  const((m, 128)), const((256, 256))]
                 + [page_spec(s) for s in range(n_slots)] * 2,
        out_specs=per_seq((1, m, 128)),
        scratch_shapes=[pltpu.VMEM((m, 128), F32), pltpu.VMEM((m, 128), F32)],
    )
    return pl.pallas_call(
        functools.partial(_attn_sample_kernel, n_slots=n_slots, n_steps=n_steps),
        grid_spec=grid_spec,
        out_shape=jax.ShapeDtypeStruct((n_seq, m, 128), F32),
        compiler_params=_params(32, 2),
        name="attention_sample",
    )(page_table, q8, k_new, v_new, brow, uo, *([cache_k] * n_slots), *([cache_v] * n_slots))


def _merge_kernel(x_ref, oa_ref, ob_ref, oc_ref, gates_ref, wb_ref, wo_ref, ga_ref, o_ref):
    merged = None
    for br, o_b in enumerate((oa_ref, ob_ref, oc_ref)):
        gate = jax.nn.sigmoid(gates_ref[:, br * D_MODEL:(br + 1) * D_MODEL])
        term = gate * jnp.dot(o_b[...], wb_ref[br], preferred_element_type=F32)
        merged = term if merged is None else merged + term
    y = jnp.dot(merged.astype(BF16), wo_ref[...], preferred_element_type=F32)
    o_ref[...] = x_ref[...] + ga_ref[0] * y


def merge_out(x, o_ret, o_sgu, o_sb, y, w_branch, w_out, ga, *, tm, rows_per_mod):
    t, d = x.shape
    r = ga.shape[1]
    tiles_per_mod = rows_per_mod // tm
    row = lambda w: pl.BlockSpec((tm, w), lambda i: (i, 0))
    return pl.pallas_call(
        _merge_kernel,
        grid=(t // tm,),
        in_specs=[row(d), row(512), row(512), row(512), row(N_GATE_COLS),
                  pl.BlockSpec((3, BRANCH_W, d), lambda i: (0, 0, 0)),
                  pl.BlockSpec((d, d), lambda i: (0, 0)),
                  pl.BlockSpec((1, r, d), lambda i: (i // tiles_per_mod, 0, 0))],
        out_specs=row(d),
        out_shape=jax.ShapeDtypeStruct((t, d), F32),
        compiler_params=_params(48, 1),
        name="merge_out",
    )(x, o_ret, o_sgu, o_sb, y, w_branch, w_out, ga)


def _topk_rows(problems, k, tie_safe):
    vals = [v for v, _ in problems]
    ids = [i for _, i in problems]
    ranks = [jnp.full(v.shape, NOT_RANKED, F32) for v in vals]
    tops = [[] for _ in problems]
    for a in range(k):
        ms = [jnp.max(v, axis=0, keepdims=True) for v in vals]
        if tie_safe:
            firsts = [jnp.min(jnp.where(v == m, i, 1e9), axis=0, keepdims=True) for v, m, i in zip(vals, ms, ids)]
            hits = [i == f for i, f in zip(ids, firsts)]
        else:
            hits = [v == m for v, m in zip(vals, ms)]
        ranks = [jnp.where(h, float(a), r) for h, r in zip(hits, ranks)]
        vals = [jnp.where(h, -jnp.inf, v) for h, v in zip(hits, vals)]
        for t, m in zip(tops, ms):
            t.append(m)
    taken = [jnp.sum(jnp.where(v == -jnp.inf, 1.0, 0.0), axis=0, keepdims=True) for v in vals]
    return list(zip(tops, ranks, vals, taken))


def _select_candidates(s1, s2, tie_safe):
    tt = s1.shape[1]
    key_id = lax.broadcasted_iota(jnp.int32, (PEER_NKEYS, tt), 0).astype(F32)
    (top1, rank1, _, taken1), (top2_rows, rank2, _, taken2) = _topk_rows(
        [(s1, key_id), (s2, key_id)], PEER_TOPK, tie_safe)
    top_id = lax.broadcasted_iota(jnp.int32, (PEER_TOPK, tt), 0)
    t1 = jnp.zeros((PEER_TOPK, tt), F32)
    t2 = jnp.zeros((PEER_TOPK, tt), F32)
    for r in range(PEER_TOPK):
        t1 = jnp.where(top_id == r, top1[r], t1)
        t2 = jnp.where(top_id == r, top2_rows[r], t2)
    half = PEER_TOPK // 2
    rank_id = top_id.astype(F32)
    slabs = [t1 + top2_rows[0], top1[0] + t2[half:]]
    ids = [rank_id * PEER_TOPK, rank_id[half:]]
    for b in range(1, half):
        slabs.append(t1[:half] + top2_rows[b])
        ids.append(rank_id[:half] * PEER_TOPK + float(b))
    state = dict(s1=s1, s2=s2, top1=top1, top2_rows=top2_rows, rank1=rank1, rank2=rank2, t1=t1, t2=t2,
                 taken=taken1 + taken2)
    return state, (jnp.concatenate(slabs, axis=0), jnp.concatenate(ids, axis=0))


def _select_finish(state, left, taken3):
    s1, s2, top1, top2_rows = state["s1"], state["s2"], state["top1"], state["top2_rows"]
    rank1, rank2, t1, t2 = state["rank1"], state["rank2"], state["t1"], state["t2"]
    tt = s1.shape[1]
    half = PEER_TOPK // 2
    chosen = jnp.where(left == -jnp.inf, 1.0, 0.0)
    ch_b0 = chosen[:PEER_TOPK]
    ch_a0 = chosen[PEER_TOPK:PEER_TOPK + half]
    e1_top = jnp.exp(t1 - top1[0])
    e2_top = jnp.exp(t2 - top2_rows[0])
    inner_cnt = jnp.zeros((half, tt), F32)
    inner_e2 = jnp.zeros((half, tt), F32)
    for b in range(1, half):
        ch = chosen[PEER_TOPK + b * half:PEER_TOPK + (b + 1) * half]
        inner_cnt = inner_cnt + ch
        inner_e2 = inner_e2 + ch * e2_top[b:b + 1]
    tail_cnt = jnp.sum(ch_a0, axis=0, keepdims=True)
    z = (jnp.sum(ch_b0 * e1_top, axis=0, keepdims=True) + jnp.sum(ch_a0 * e2_top[half:], axis=0, keepdims=True)
         + jnp.sum(inner_e2 * e1_top[:half], axis=0, keepdims=True))
    counts = [ch_b0[a:a + 1] + (inner_cnt[a:a + 1] if a < half else 0.0) + (tail_cnt if a == 0 else 0.0)
              for a in range(PEER_TOPK)]
    c1 = jnp.zeros((PEER_NKEYS, tt), F32)
    for a in range(PEER_TOPK):
        c1 = jnp.where(rank1 == float(a), counts[a], c1)
    f1 = jnp.exp(s1 - top1[0]) / z
    return c1, f1, rank2, jnp.exp(s2 - top2_rows[0]), state["taken"] + taken3


def _peer_select_chunks(scores, tie_safe):
    states, cands = zip(*[_select_candidates(s1, s2, tie_safe) for s1, s2 in scores])
    joint = _topk_rows(list(cands), PEER_TOPK, tie_safe)
    return [_select_finish(state, left, taken3) for state, (_, _, left, taken3) in zip(states, joint)]


def _peer_select_kernel(q_ref, sk_ref, c1_ref, f1_ref, r2_ref, f2_ref):
    tt = q_ref.shape[0]
    grouped = (PEER_NKEYS // BF16_ROWS, BF16_ROWS, LANES)
    n_chunks = tt // LANES
    for c0 in range(0, n_chunks, 2):
        toks = [slice(c * LANES, (c + 1) * LANES) for c in range(c0, min(c0 + 2, n_chunks))]
        scores = [[_dot_nt(sk_ref[p], q_ref[tok, p * 128:(p + 1) * 128], precision=lax.Precision.HIGHEST)
                   for p in range(2)] for tok in toks]

        def write(results, toks=toks):
            for tok, (c1, f1, r2, f2, _) in zip(toks, results):
                c1_ref[0, :, tok] = c1
                f1_ref[0, :, tok] = f1
                r2_ref[0, :, :, tok] = r2.astype(BF16).reshape(grouped)
                f2_ref[0, :, :, tok] = f2.astype(BF16).reshape(grouped)

        fast = _peer_select_chunks(scores, tie_safe=False)
        write(fast)
        tied = sum(jnp.sum(jnp.where(taken != 3.0 * PEER_TOPK, 1.0, 0.0)) for *_, taken in fast)

        @pl.when(tied > 0.0)
        def _(scores=scores, write=write):
            write(_peer_select_chunks(scores, tie_safe=True))


def peer_select(q, sub_keys, tt):
    t = q.shape[0]
    n_groups = PEER_NKEYS // BF16_ROWS
    shape1 = jax.ShapeDtypeStruct((PEER_HEADS, PEER_NKEYS, t), F32)
    shape2 = jax.ShapeDtypeStruct((PEER_HEADS, n_groups, BF16_ROWS, t), BF16)
    spec1 = pl.BlockSpec((1, PEER_NKEYS, tt), lambda i, h: (h, 0, i))
    spec2 = pl.BlockSpec((1, n_groups, BF16_ROWS, tt), lambda i, h: (h, 0, 0, i))
    return pl.pallas_call(
        _peer_select_kernel,
        grid=(t // tt, PEER_HEADS),
        in_specs=[pl.BlockSpec((tt, 256), lambda i, h: (i, h)),
                  pl.BlockSpec((2, PEER_NKEYS, 128), lambda i, h: (0, 0, 0))],
        out_specs=[spec1, spec1, spec2, spec2], out_shape=[shape1, shape1, shape2, shape2],
        compiler_params=_params(32, 2),
        name="peer_select",
    )(q, sub_keys)


def _gelu_tanh(x):
    k = -2.0 * 0.7978845608028654 * LOG2E
    return x / (1.0 + jnp.exp2(x * (k + (k * 0.044715) * (x * x))))


def _peer_dense_kernel(h_ref, u_ref, vt_ref, c1_ref, f1_ref, r2_ref, f2_ref, x_ref, ga_ref, o_ref,
                       acc, act, wact, *, n_rows, n_steps):
    e = pl.program_id(1)
    tt = h_ref.shape[0]

    @pl.when(e == 0)
    def _():
        acc[...] = jnp.zeros_like(acc)

    chunk = min(tt, 2 * LANES)
    zero = jnp.zeros((BF16_ROWS, chunk), BF16)
    n_groups = PEER_NKEYS // BF16_ROWS
    te = u_ref.shape[0]
    rows_per_part = n_rows // N_DENSE_PARTS
    tp = rows_per_part * PEER_NKEYS
    gp = rows_per_part * n_groups

    def project(p):
        return _dot_nt(u_ref[p * tp:(p + 1) * tp, :], h_ref[...])

    def combine(p):
        return jnp.dot(vt_ref[:, p * tp:(p + 1) * tp], wact[p * gp:(p + 1) * gp].reshape(tp, tt),
                       preferred_element_type=F32)

    raw_next = project(0)
    for p in range(N_DENSE_PARTS):
        raw = raw_next
        if p + 1 < N_DENSE_PARTS:
            raw_next = project(p + 1)
        act[p * gp:(p + 1) * gp] = _gelu_tanh(raw).astype(BF16).reshape(gp, BF16_ROWS, tt)
        if p > 0:
            acc[...] += combine(p - 1)
        for il in range(p * rows_per_part, (p + 1) * rows_per_part):
            for tc in range(tt // chunk):
                lanes = slice(tc * chunk, (tc + 1) * chunk)
                w = [zero] * n_groups
                for h in range(PEER_HEADS):
                    c1 = jnp.broadcast_to(c1_ref[h, il:il + 1, lanes], (BF16_ROWS, chunk)).astype(BF16)
                    f1 = jnp.broadcast_to(f1_ref[h, il:il + 1, lanes], (BF16_ROWS, chunk)).astype(BF16)
                    for g in range(n_groups):
                        w[g] = w[g] + jnp.where(r2_ref[h, g, :, lanes] < c1, f2_ref[h, g, :, lanes], zero) * f1
                for g in range(n_groups):
                    wact[il * n_groups + g, :, lanes] = w[g] * act[il * n_groups + g, :, lanes]
    acc[...] += combine(N_DENSE_PARTS - 1)

    @pl.when(e == n_steps - 1)
    def _():
        o_ref[...] = x_ref[...] + ga_ref[0] * acc[...].T


def peer_dense(x, h2, exp_u, exp_vt, sel, ga, *, tt, rows_per_mod, n_rows=8):
    t, d = x.shape
    n_exp = exp_u.shape[0]
    te = n_rows * PEER_NKEYS
    n_steps = n_exp // te
    r = ga.shape[1]
    tiles_per_mod = rows_per_mod // tt
    row_spec = pl.BlockSpec((PEER_HEADS, n_rows, tt), lambda i, e: (0, e, i))
    n_groups = PEER_NKEYS // BF16_ROWS
    full_spec = pl.BlockSpec((PEER_HEADS, n_groups, BF16_ROWS, tt), lambda i, e: (0, 0, 0, i))
    grouped = pltpu.VMEM((te // BF16_ROWS, BF16_ROWS, tt), BF16)
    return pl.pallas_call(
        functools.partial(_peer_dense_kernel, n_rows=n_rows, n_steps=n_steps),
        grid=(t // tt, n_steps),
        in_specs=[pl.BlockSpec((tt, d), lambda i, e: (i, 0)),
                  pl.BlockSpec((te, d), lambda i, e: (e, 0)),
                  pl.BlockSpec((d, te), lambda i, e: (0, e)),
                  row_spec, row_spec, full_spec, full_spec,
                  pl.BlockSpec((tt, d), lambda i, e: (i, 0)),
                  pl.BlockSpec((1, r, d), lambda i, e: (i // tiles_per_mod, 0, 0))],
        out_specs=pl.BlockSpec((tt, d), lambda i, e: (i, 0)),
        out_shape=jax.ShapeDtypeStruct((t, d), F32),
        scratch_shapes=[pltpu.VMEM((d, tt), F32), grouped, grouped],
        compiler_params=_params(56, 2),
        name="peer_dense",
    )(h2, exp_u, exp_vt, *sel, x, ga)


def _token_tile(rows_per_mod):
    return 512 if rows_per_mod % 512 == 0 else rows_per_mod


def _layer(x, mods, lw, *, n_batch, sample):
    t, d = x.shape
    l = t // n_batch
    sh1, sc1, ga1, sh2, sc2, ga2 = mods
    rows_per_mod = l if sample is None else t
    tm = _token_tile(rows_per_mod)
    tm_in = 2 * tm if rows_per_mod % (2 * tm) == 0 else tm
    y = norm_mod_matmul(x, lw["g_norm1"], sc1, sh1, lw["w_in"], tm=tm_in, tn=1536, rows_per_mod=rows_per_mod)

    qn, kn, knb, vb = qk_prep(y, lw["g_q"], lw["g_k"], tm)
    if sample is None:
        y3 = y.reshape(n_batch, l, -1)
        o_ret, s_new = retention_prompt(y3, lw["rope_cos"], lw["rope_sin"], lw["g_ret"])
        o_sgu, vn_rows = spatial_gating(y3, lw["g_sgu"], lw["w_mix"], lw["b_mix"], emit_vn=False)
        o_sb = attention_prompt(qn.reshape(n_batch, l, 512), knb.reshape(n_batch, l, 512),
                                vb.reshape(n_batch, l, 512), lw["b_sb"], lw["uo"])
    else:
        y3 = y.reshape(1, t, -1)
        o_ret, s_new = retention_sample(y3, lw["rope_cos"], lw["rope_sin"], lw["g_ret"],
                                        sample["state_ret"], sample["layer"], l)
        o_sgu, vn_rows = spatial_gating(y3, lw["g_sgu"], lw["w_mix"], lw["b_mix"], emit_vn=True)
        def head_major(a, rows):
            a = jnp.transpose(a.reshape(n_batch, l, H_SB, D_SB), (0, 2, 1, 3))
            return jnp.pad(a, ((0, 0), (0, 0), (0, rows - l), (0, 0)))

        q8 = head_major(qn, ROWS_PER_HEAD).reshape(n_batch, H_SB * ROWS_PER_HEAD, D_SB)
        brow = jnp.broadcast_to(jnp.repeat(lw["b_sb"] * LOG2E, ROWS_PER_HEAD)[:, None],
                                (H_SB * ROWS_PER_HEAD, 128))
        o8 = attention_sample(q8, head_major(knb, 128), head_major(vb, 128), brow, lw["uo"],
                              sample["cache_k"], sample["cache_v"], sample["page_table"], sample["layer"])
        o8 = o8.reshape(n_batch, H_SB, ROWS_PER_HEAD, D_SB)[:, :, :l]
        o_sb = jnp.transpose(o8, (0, 2, 1, 3)).reshape(t, 512).astype(BF16)

    x = merge_out(x, o_ret.reshape(t, 512), o_sgu.reshape(t, 512), o_sb.reshape(t, 512), y,
                  lw["w_branch"], lw["w_out"], ga1, tm=tm, rows_per_mod=rows_per_mod)

    pq, h2 = norm_mod_matmul(x, lw["g_norm2"], sc2, sh2, lw["w_pq"], tm=tm, tn=512,
                             rows_per_mod=rows_per_mod, emit_h=True, precise=True)
    tt = tm
    sel = peer_select(pq, lw["sub_keys"], tt)
    x = peer_dense(x, h2, lw["exp_u"], lw["exp_vt"], sel, ga2, tt=tt, rows_per_mod=rows_per_mod)
    v_raw = y[:, CB_BV * 512:(CB_BV + 1) * 512]
    return x, s_new, kn, v_raw, vn_rows


def kernel(x_prompt, x_sample, cache_k, cache_v, state_ret, page_table, c_prompt, c_sample, w_ada, b_ada,
           g_norm1, g_norm2, w_in, g_ret, g_q, g_k, b_sb, g_sgu, w_sp, b_sp, w_branch, w_out, w_pq,
           sub_keys, exp_u, exp_v):
    n_b, seq, d = x_prompt.shape
    n_s, dec, _ = x_sample.shape
    depth = w_in.shape[0]
    n_pool, _, page, _, _ = cache_k.shape
    past_len = page_table.shape[1] * page
    t_s = n_s * dec

    mod = ada_mod(jnp.concatenate([c_prompt, c_sample], axis=0), w_ada, b_ada)
    cache_k4 = cache_k.reshape(n_pool, depth, page * H_SB, D_SB)
    cache_v4 = cache_v.reshape(n_pool, depth, page * H_SB, D_SB)
    uo = _cumsum_weights()
    cos_p, sin_p = _rope_tables(jnp.arange(seq))
    cos_s, sin_s = _rope_tables(past_len + (jnp.arange(t_s) % dec))

    pos = np.arange(t_s)
    same_seq = jnp.asarray((pos[:, None] // dec) == (pos[None, :] // dec))
    tri = jnp.asarray(np.tril(np.ones((SGU_CHUNK, SGU_CHUNK), bool)))

    xp = x_prompt.reshape(n_b * seq, d)
    xs = x_sample.reshape(t_s, d)
    outs = {k: [] for k in ("kp", "vp", "ks", "vs", "rp", "rs", "us")}
    for l in range(depth):
        w_in_l = jnp.concatenate([w_in[l][:, -N_GATE_COLS:], w_in[l][:, :-N_GATE_COLS]], axis=1).astype(BF16)
        shared = dict(g_norm1=g_norm1[l], g_norm2=g_norm2[l], w_in=w_in_l, g_ret=g_ret[l], g_q=g_q[l], g_k=g_k[l],
                      b_sb=b_sb[l], g_sgu=g_sgu[l], w_branch=w_branch[l].astype(BF16), w_out=w_out[l].astype(BF16),
                      w_pq=_split_bf16(w_pq[l]), sub_keys=sub_keys[l], exp_u=exp_u[l].astype(BF16),
                      exp_vt=exp_v[l].T.astype(BF16), uo=uo)
        mods_p = [m.reshape(n_b, 1, d) for m in jnp.split(mod[l, :n_b], 6, axis=-1)]
        mods_s = [jnp.repeat(m, dec, axis=0).reshape(1, t_s, d) for m in jnp.split(mod[l, n_b:], 6, axis=-1)]

        w_tri = jnp.where(tri[None], w_sp[l], 0.0)
        lw_p = dict(shared, rope_cos=cos_p, rope_sin=sin_p, w_mix=w_tri.astype(BF16),
                    b_mix=jnp.broadcast_to(b_sp[l][:, :, None], (G_SGU, SGU_CHUNK, 128)))
        xp, rp, kp, vp, _ = _layer(xp, mods_p, lw_p, n_batch=n_b, sample=None)

        w_blk = jnp.where(same_seq[None], jnp.tile(w_tri[:, :dec, :dec], (1, n_s, n_s)), 0.0)
        b_blk = jnp.tile(b_sp[l][:, :dec], (1, n_s))
        lw_s = dict(shared, rope_cos=cos_s, rope_sin=sin_s, w_mix=w_blk.astype(BF16),
                    b_mix=jnp.broadcast_to(b_blk[:, :, None], (G_SGU, t_s, 128)))
        sample = dict(state_ret=state_ret, cache_k=cache_k4, cache_v=cache_v4, page_table=page_table, layer=l)
        xs, rs, ks, vs, us = _layer(xs, mods_s, lw_s, n_batch=n_s, sample=sample)

        outs["kp"].append(kp.reshape(n_b, seq, H_SB, D_SB))
        outs["vp"].append(vp.reshape(n_b, seq, H_SB, D_SB))
        outs["ks"].append(ks.reshape(n_s, dec, H_SB, D_SB))
        outs["vs"].append(vs.reshape(n_s, dec, H_SB, D_SB))
        outs["rp"].append(rp)
        outs["rs"].append(rs)
        outs["us"].append(us.reshape(n_s, dec, -1))
    stack = lambda name: jnp.stack(outs[name], axis=1)
    return (xp.reshape(n_b, seq, d), xs.reshape(n_s, dec, d),
            stack("kp"), stack("vp"), stack("ks"), stack("vs"), stack("rp"), stack("rs"), stack("us"))
```

```python
import functools

import numpy as np
import jax
import jax.numpy as jnp
from jax import lax
from jax.experimental import pallas as pl
from jax.experimental.pallas import tpu as pltpu

F32 = jnp.float32
BF16 = jnp.bfloat16

D_MODEL = 1024
H_RET = 4
DK_RET = 128
RET_CHUNK = 128
ROPE_BASE = 10000.0
G_SGU = 4
SGU_CHUNK = 128
H_SB = 4
D_SB = 128
BRANCH_W = 512
PEER_HEADS = 8
PEER_NKEYS = 128
PEER_TOPK = 16
EPS = 1e-6

LOG2E = 1.4426950408889634
Q_LOGIT_SCALE = D_SB ** -0.5 * LOG2E
LANES = 128
BF16_ROWS = 16
N_DENSE_PARTS = 4
MIB = 2 ** 20
NOT_RANKED = 99.0
N_GATE_COLS = 3 * D_MODEL
CB_RQ, CB_RK, CB_RV, CB_RG, CB_SU, CB_SV, CB_BQ, CB_BK, CB_BV = range(6, 15)


def _params(vmem_mib, n_grid):
    return pltpu.CompilerParams(dimension_semantics=("arbitrary",) * n_grid,
                                vmem_limit_bytes=vmem_mib * MIB)


def _ada_kernel(c_ref, w_ref, b_ref, o_ref):
    c = c_ref[...]
    s = (c * jax.nn.sigmoid(c)).astype(BF16)
    o_ref[0] = jnp.dot(s, w_ref[0].astype(BF16), preferred_element_type=F32) + b_ref[0]


def ada_mod(c_all, w_ada, b_ada):
    depth, d, n = w_ada.shape
    m = c_all.shape[0]
    tn = 1536
    return pl.pallas_call(
        _ada_kernel,
        grid=(depth, n // tn),
        in_specs=[pl.BlockSpec((m, d), lambda l, j: (0, 0)),
                  pl.BlockSpec((1, d, tn), lambda l, j: (l, 0, j)),
                  pl.BlockSpec((1, 1, tn), lambda l, j: (l, 0, j))],
        out_specs=pl.BlockSpec((1, m, tn), lambda l, j: (l, 0, j)),
        out_shape=jax.ShapeDtypeStruct((depth, m, n), F32),
        compiler_params=_params(40, 2),
        name="ada_mod",
    )(c_all, w_ada, b_ada.reshape(depth, 1, n))


def _nmm_kernel(x_ref, g_ref, sc_ref, sh_ref, *rest, emit_h, precise):
    n_w = 2 if precise else 1
    w_refs, y_ref, scratch = rest[:n_w], rest[n_w], rest[-n_w:]

    @pl.when(pl.program_id(1) == 0)
    def _():
        x = x_ref[...]
        ms = jnp.mean(x * x, axis=-1, keepdims=True)
        y = x * lax.rsqrt(ms + EPS) * g_ref[...]
        h = y * (1.0 + sc_ref[0]) + sh_ref[0]
        h_hi = h.astype(BF16)
        scratch[0][...] = h_hi
        if precise:
            scratch[1][...] = (h - h_hi.astype(F32)).astype(BF16)
        if emit_h:
            rest[n_w + 1][...] = h_hi

    dot = functools.partial(jnp.dot, preferred_element_type=F32)
    if precise:
        h_hi, h_lo = scratch[0][...], scratch[1][...]
        y_ref[...] = dot(h_hi, w_refs[0][...]) + (dot(h_hi, w_refs[1][...]) + dot(h_lo, w_refs[0][...]))
    else:
        y_ref[...] = dot(scratch[0][...], w_refs[0][...])


def _split_bf16(w):
    hi = w.astype(BF16)
    return hi, (w - hi.astype(F32)).astype(BF16)


def norm_mod_matmul(x, g, sc, sh, w, *, tm, tn, rows_per_mod, emit_h=False, precise=False):
    t, d = x.shape
    ws = tuple(w) if precise else (w,)
    n = ws[0].shape[1]
    r = sc.shape[1]
    tiles_per_mod = rows_per_mod // tm
    mod_spec = pl.BlockSpec((1, r, d), lambda i, j: (i // tiles_per_mod, 0, 0))
    out_shape = [jax.ShapeDtypeStruct((t, n), F32)]
    out_specs = [pl.BlockSpec((tm, tn), lambda i, j: (i, j))]
    if emit_h:
        out_shape.append(jax.ShapeDtypeStruct((t, d), BF16))
        out_specs.append(pl.BlockSpec((tm, d), lambda i, j: (i, 0)))
    res = pl.pallas_call(
        functools.partial(_nmm_kernel, emit_h=emit_h, precise=precise),
        grid=(t // tm, n // tn),
        in_specs=[pl.BlockSpec((tm, d), lambda i, j: (i, 0)),
                  pl.BlockSpec((1, d), lambda i, j: (0, 0)),
                  mod_spec, mod_spec] + [pl.BlockSpec((d, tn), lambda i, j: (0, j))] * len(ws),
        out_specs=out_specs,
        out_shape=out_shape,
        scratch_shapes=[pltpu.VMEM((tm, d), BF16)] * len(ws),
        compiler_params=_params(48, 2),
        name="norm_mod_matmul",
    )(x, g.reshape(1, d), sc, sh, *ws)
    return res if emit_h else res[0]


def _rope(x, cos_full, sin_signed):
    return x * cos_full + pltpu.roll(x, 64, 1) * sin_signed


def _ret_gamma_logs():
    return np.log(1.0 - np.exp2(-5.0 - np.arange(H_RET, dtype=np.float64)))


def _dot_t(a, b):
    return lax.dot_general(a, b, (((0,), (0,)), ((), ())), preferred_element_type=F32)


def _dot_nt(a, b, **kw):
    return lax.dot_general(a, b, (((1,), (1,)), ((), ())), preferred_element_type=F32, **kw)


def _ret_head_qkv(q_ref, k_ref, v_ref, cos, sin, h):
    sl = slice(h * 128, (h + 1) * 128)
    q = _rope(q_ref[0, :, sl], cos, sin)
    k = _rope(k_ref[0, :, sl], cos, sin) * (DK_RET ** -0.5)
    v = v_ref[0, :, sl]
    return q, k, v


def _ret_finish(o, gate, g_row):
    mu = jnp.mean(o, axis=-1, keepdims=True)
    xc = o - mu
    var = jnp.mean(xc * xc, axis=-1, keepdims=True)
    y = xc * lax.rsqrt(var + EPS) * g_row
    return gate * jax.nn.sigmoid(gate) * y


def _ret_prompt_kernel(q_ref, k_ref, v_ref, g_ref, cos_ref, sin_ref, dm_ref, qd_ref, kd_ref, gr_ref,
                       o_ref, sfin_ref, state, *, chunk_dec, n_chunks):
    c = pl.program_id(1)

    @pl.when(c == 0)
    def _():
        state[...] = jnp.zeros_like(state)

    cos = cos_ref[...]
    sin = sin_ref[...]
    for h in range(H_RET):
        sl = slice(h * 128, (h + 1) * 128)
        q, k, v = _ret_head_qkv(q_ref, k_ref, v_ref, cos, sin, h)
        vb = v.astype(BF16)
        scores = _dot_nt(q.astype(BF16), k.astype(BF16)) * dm_ref[h]
        s_prev = state[h]
        o = jnp.dot(scores.astype(BF16), vb, preferred_element_type=F32)
        o = o + jnp.dot((q * qd_ref[:, sl]).astype(BF16), s_prev.astype(BF16), preferred_element_type=F32)
        state[h] = chunk_dec[h] * s_prev + _dot_t((k * kd_ref[:, sl]).astype(BF16), vb)
        o_ref[0, :, sl] = _ret_finish(o, g_ref[0, :, sl], gr_ref[:, sl]).astype(o_ref.dtype)

    @pl.when(c == n_chunks - 1)
    def _():
        sfin_ref[0] = state[...]


def _ret_tables(cs, rows):
    lg = _ret_gamma_logs()
    i = np.arange(rows) % cs
    seq = np.arange(rows) // cs
    diff = i[:, None] - i[None, :]
    same = seq[:, None] == seq[None, :]
    dmask = np.where((diff >= 0)[None] & same[None], np.exp(diff[None] * lg[:, None, None]), 0.0)
    k_dec = np.exp((cs - 1 - i)[:, None] * lg[None, :])
    q_dec = np.exp((i + 1)[:, None] * lg[None, :])
    k_dec = np.repeat(k_dec, 128, axis=1)
    q_dec = np.repeat(q_dec, 128, axis=1)
    chunk_dec = tuple(float(x) for x in np.exp(cs * lg))
    return (jnp.asarray(dmask, F32), jnp.asarray(q_dec, F32), jnp.asarray(k_dec, F32), chunk_dec)


def _rope_tables(pos):
    half = DK_RET // 2
    inv = ROPE_BASE ** (-jnp.arange(half, dtype=F32) / half)
    ang = pos.astype(F32)[:, None] * inv[None, :]
    cos = jnp.cos(ang)
    sin = jnp.sin(ang)
    return jnp.concatenate([cos, cos], axis=-1), jnp.concatenate([-sin, sin], axis=-1)


def retention_prompt(y3, rope_cos, rope_sin, g_ret):
    b, l, _ = y3.shape
    cs = RET_CHUNK
    nc = l // cs
    dmask, q_dec, k_dec, chunk_dec = _ret_tables(cs, cs)

    def col(cb):
        return pl.BlockSpec((1, cs, 512), lambda bi, c: (bi, c, cb))

    const2 = lambda shape: pl.BlockSpec(shape, lambda bi, c: (0,) * len(shape))
    return pl.pallas_call(
        functools.partial(_ret_prompt_kernel, chunk_dec=chunk_dec, n_chunks=nc),
        grid=(b, nc),
        in_specs=[col(CB_RQ), col(CB_RK), col(CB_RV), col(CB_RG),
                  pl.BlockSpec((cs, 128), lambda bi, c: (c, 0)),
                  pl.BlockSpec((cs, 128), lambda bi, c: (c, 0)),
                  const2((H_RET, cs, cs)), const2((cs, 512)), const2((cs, 512)), const2((1, 512))],
        out_specs=[pl.BlockSpec((1, cs, 512), lambda bi, c: (bi, c, 0)),
                   pl.BlockSpec((1, H_RET, 128, 128), lambda bi, c: (bi, 0, 0, 0))],
        out_shape=[jax.ShapeDtypeStruct((b, l, 512), BF16),
                   jax.ShapeDtypeStruct((b, H_RET, 128, 128), F32)],
        scratch_shapes=[pltpu.VMEM((H_RET, 128, 128), F32)],
        compiler_params=_params(32, 2),
        name="retention_prompt",
    )(y3, y3, y3, y3, rope_cos, rope_sin, dmask, q_dec, k_dec, g_ret.reshape(1, 512))


def _ret_sample_kernel(q_ref, k_ref, v_ref, g_ref, cos_ref, sin_ref, dm_ref, qd_ref, kd_ref, gr_ref, s0_ref,
                       o_ref, sfin_ref, o_state, *, chunk_dec, n_seq, cs):
    b = pl.program_id(0)

    @pl.when(b == 0)
    def _():
        o_state[...] = jnp.zeros_like(o_state)

    cos = cos_ref[...]
    sin = sin_ref[...]
    rows = q_ref.shape[1]
    in_seq = lax.shift_right_logical(lax.broadcasted_iota(jnp.int32, (rows, 128), 0), cs.bit_length() - 1) == b
    for h in range(H_RET):
        sl = slice(h * 128, (h + 1) * 128)
        q, k, v = _ret_head_qkv(q_ref, k_ref, v_ref, cos, sin, h)
        s_prev = s0_ref[0, 0, h]
        qm = jnp.where(in_seq, q * qd_ref[:, sl], 0.0)
        km = jnp.where(in_seq, k * kd_ref[:, sl], 0.0)
        o_state[:, sl] += jnp.dot(qm.astype(BF16), s_prev.astype(BF16), preferred_element_type=F32)
        sfin_ref[0, h] = chunk_dec[h] * s_prev + _dot_t(km.astype(BF16), v.astype(BF16))

    @pl.when(b == n_seq - 1)
    def _():
        for h in range(H_RET):
            sl = slice(h * 128, (h + 1) * 128)
            q, k, v = _ret_head_qkv(q_ref, k_ref, v_ref, cos, sin, h)
            scores = _dot_nt(q.astype(BF16), k.astype(BF16)) * dm_ref[h]
            o = jnp.dot(scores.astype(BF16), v.astype(BF16), preferred_element_type=F32) + o_state[:, sl]
            o_ref[0, :, sl] = _ret_finish(o, g_ref[0, :, sl], gr_ref[:, sl]).astype(o_ref.dtype)


def retention_sample(y3, rope_cos, rope_sin, g_ret, state_ret, layer, cs):
    _, t, _ = y3.shape
    n_seq = t // cs
    dmask, q_dec, k_dec, chunk_dec = _ret_tables(cs, t)

    def col(cb):
        return pl.BlockSpec((1, t, 512), lambda bi: (0, 0, cb))

    const = lambda shape: pl.BlockSpec(shape, lambda bi: (0,) * len(shape))
    return pl.pallas_call(
        functools.partial(_ret_sample_kernel, chunk_dec=chunk_dec, n_seq=n_seq, cs=cs),
        grid=(n_seq,),
        in_specs=[col(CB_RQ), col(CB_RK), col(CB_RV), col(CB_RG),
                  const((t, 128)), const((t, 128)),
                  const((H_RET, t, t)), const((t, 512)), const((t, 512)), const((1, 512)),
                  pl.BlockSpec((1, 1, H_RET, 128, 128), lambda bi: (bi, layer, 0, 0, 0))],
        out_specs=[pl.BlockSpec((1, t, 512), lambda bi: (0, 0, 0)),
                   pl.BlockSpec((1, H_RET, 128, 128), lambda bi: (bi, 0, 0, 0))],
        out_shape=[jax.ShapeDtypeStruct((1, t, 512), BF16),
                   jax.ShapeDtypeStruct((n_seq, H_RET, 128, 128), F32)],
        scratch_shapes=[pltpu.VMEM((t, 512), F32)],
        compiler_params=_params(32, 1),
        name="retention_sample",
    )(y3, y3, y3, y3, rope_cos, rope_sin, dmask, q_dec, k_dec, g_ret.reshape(1, 512), state_ret)


def _sgu_kernel(u_ref, v_ref, g_ref, w_ref, b_ref, o_ref, *rest, emit_vn):
    v = v_ref[0]
    ms = jnp.mean(v * v, axis=-1, keepdims=True)
    vn = v * lax.rsqrt(ms + EPS) * g_ref[...]
    if emit_vn:
        rest[0][0] = vn
    vb = vn.astype(BF16)
    for g in range(G_SGU):
        sl = slice(g * 128, (g + 1) * 128)
        mixed = jnp.dot(w_ref[g], vb[:, sl], preferred_element_type=F32) + b_ref[g]
        o_ref[0, :, sl] = (u_ref[0, :, sl] * mixed).astype(o_ref.dtype)


def spatial_gating(y3, g_sgu, w_mix, b_mix, emit_vn):
    b, l, _ = y3.shape
    rows = SGU_CHUNK
    nc = l // rows

    def col(cb):
        return pl.BlockSpec((1, rows, 512), lambda bi, c: (bi, c, cb))

    const = lambda shape: pl.BlockSpec(shape, lambda bi, c: (0,) * len(shape))
    out_spec = pl.BlockSpec((1, rows, 512), lambda bi, c: (bi, c, 0))
    out_shape = [jax.ShapeDtypeStruct((b, l, 512), BF16)]
    out_specs = [out_spec]
    if emit_vn:
        out_shape.append(jax.ShapeDtypeStruct((b, l, 512), F32))
        out_specs.append(out_spec)
    res = pl.pallas_call(
        functools.partial(_sgu_kernel, emit_vn=emit_vn),
        grid=(b, nc),
        in_specs=[col(CB_SU), col(CB_SV), const((1, 512)),
                  const((G_SGU, rows, rows)), const((G_SGU, rows, 128))],
        out_specs=out_specs, out_shape=out_shape,
        compiler_params=_params(32, 2),
        name="spatial_gating",
    )(y3, y3, g_sgu.reshape(1, 512), w_mix, b_mix)
    return res if emit_vn else (res[0], None)


def _qk_prep_kernel(q_ref, k_ref, v_ref, gq_ref, gk_ref, qn_ref, kn_ref, knb_ref, vb_ref):
    for h in range(H_SB):
        sl = slice(h * 128, (h + 1) * 128)
        q = q_ref[:, sl]
        k = k_ref[:, sl]
        qn = q * lax.rsqrt(jnp.mean(q * q, axis=-1, keepdims=True) + EPS) * gq_ref[...]
        kn = k * lax.rsqrt(jnp.mean(k * k, axis=-1, keepdims=True) + EPS) * gk_ref[...]
        qn_ref[:, sl] = (qn * Q_LOGIT_SCALE).astype(BF16)
        kn_ref[:, sl] = kn
        knb_ref[:, sl] = kn.astype(BF16)
    vb_ref[...] = v_ref[...].astype(BF16)


def qk_prep(y, g_q, g_k, tm):
    t = y.shape[0]

    def col(cb):
        return pl.BlockSpec((tm, 512), lambda i: (i, cb))

    out_spec = pl.BlockSpec((tm, 512), lambda i: (i, 0))
    return pl.pallas_call(
        _qk_prep_kernel,
        grid=(t // tm,),
        in_specs=[col(CB_BQ), col(CB_BK), col(CB_BV),
                  pl.BlockSpec((1, 128), lambda i: (0, 0)), pl.BlockSpec((1, 128), lambda i: (0, 0))],
        out_specs=[out_spec] * 4,
        out_shape=[jax.ShapeDtypeStruct((t, 512), BF16), jax.ShapeDtypeStruct((t, 512), F32),
                   jax.ShapeDtypeStruct((t, 512), BF16), jax.ShapeDtypeStruct((t, 512), BF16)],
        compiler_params=_params(32, 1),
        name="qk_prep",
    )(y, y, y, g_q.reshape(1, 128), g_k.reshape(1, 128))


def _cumsum_weights():
    j = np.arange(128)
    later = (j[:, None] > j[None, :]).astype(np.float32)
    w = np.concatenate([later, np.ones((128, 128), np.float32)], axis=1)
    return jnp.asarray(np.concatenate([w, w], axis=0), BF16)


def _sb_tile(z2, carry, uo, mask):
    base, r = _sb_stage1(z2, uo, mask)
    return _sb_stage2(base, r, carry)


MASKED_LOG = -1e30


def _sb_stage1(z2, uo, mask):
    lg2 = jnp.log(1.0 + jnp.exp2(-jnp.abs(z2))) * LOG2E
    log_1m = -(jnp.maximum(z2, 0.0) + lg2)
    base = z2 + log_1m
    if mask is not None:
        log_1m = jnp.where(mask, log_1m, 0.0)
        base = jnp.where(mask, base, MASKED_LOG)
    hi = log_1m.astype(BF16)
    lo = (log_1m - hi.astype(F32)).astype(BF16)
    r = jnp.dot(jnp.concatenate([hi, lo], axis=1), uo, preferred_element_type=F32)
    return base, r


def _sb_stage2(base, r, carry):
    a = jnp.exp2(base + (r[:, :128] + carry))
    return a, carry + r[:, 128:]


def _attn_prompt_kernel(bias_ref, q_ref, k_ref, v_ref, uo_ref, o_ref, *, tq):
    h = pl.program_id(1)
    i = pl.program_id(2)
    q = q_ref[0]
    bias2 = bias_ref[h] * LOG2E
    uo = uo_ref[...]
    n_band = tq // 128
    row = lax.broadcasted_iota(jnp.int32, (tq, 128), 0)
    colk = lax.broadcasted_iota(jnp.int32, (tq, 128), 1)

    def tile_start(t):
        return pl.multiple_of(jnp.maximum((i + 1) * n_band - 1 - t, 0) * 128, 128)

    def logits(t):
        return _dot_nt(q, k_ref[0, pl.ds(tile_start(t), 128), :]) + bias2

    def finish(t_prev, acc, log_a_prev):
        v = v_ref[0, pl.ds(tile_start(t_prev), 128), :]
        return acc + jnp.dot(jnp.exp2(log_a_prev).astype(BF16), v, preferred_element_type=F32)

    def trip(t, c, mask):
        carry, acc, z2, log_a_prev = c
        z_next = logits(t + 1)
        acc = finish(jnp.maximum(t - 1, 0), acc, log_a_prev)
        base, r = _sb_stage1(z2, uo, mask)
        return carry + r[:, 128:], acc, z_next, base + (r[:, :128] + carry)

    zero = jnp.zeros((tq, 128), F32)
    c = (zero, zero, logits(0), jnp.full((tq, 128), MASKED_LOG, F32))
    for t in range(n_band):
        c = trip(t, c, colk + (n_band - 1 - t) * 128 < row)
    c = lax.fori_loop(n_band, (i + 1) * n_band, lambda t, c: trip(t, c, None), c)
    acc = finish((i + 1) * n_band - 1, c[1], c[3])
    o_ref[0] = acc.astype(o_ref.dtype)


def attention_prompt(qn, knb, vb, b_sb, uo, tq=512):
    b, l, _ = qn.shape
    tq = min(tq, l)
    return pl.pallas_call(
        functools.partial(_attn_prompt_kernel, tq=tq),
        grid=(b, H_SB, l // tq),
        in_specs=[pl.BlockSpec(memory_space=pltpu.SMEM),
                  pl.BlockSpec((1, tq, 128), lambda bi, h, i: (bi, i, h)),
                  pl.BlockSpec((1, l, 128), lambda bi, h, i: (bi, 0, h)),
                  pl.BlockSpec((1, l, 128), lambda bi, h, i: (bi, 0, h)),
                  pl.BlockSpec((256, 256), lambda bi, h, i: (0, 0))],
        out_specs=pl.BlockSpec((1, tq, 128), lambda bi, h, i: (bi, i, h)),
        out_shape=jax.ShapeDtypeStruct((b, l, 512), BF16),
        compiler_params=_params(32, 3),
        name="attention_prompt",
    )(b_sb, qn, knb, vb, uo)


ROWS_PER_HEAD = 8


def _attn_sample_kernel(pt_ref, q_ref, kn_ref, vn_ref, brow_ref, uo_ref, *rest, n_slots, n_steps):
    kp = rest[:n_slots]
    vp = rest[n_slots:2 * n_slots]
    o_ref, carry_s, acc_s = rest[2 * n_slots:]
    j = pl.program_id(1)
    uo = uo_ref[...]
    brow = brow_ref[...]
    heads = range(H_SB)
    rows = [slice(h * ROWS_PER_HEAD, (h + 1) * ROWS_PER_HEAD) for h in heads]
    qs = [q_ref[0, rows[h], :] for h in heads]

    def logits(k_of_head):
        return jnp.concatenate([_dot_nt(qs[h], k_of_head(h)) for h in heads], axis=0) + brow

    def accumulate(a, v_of_head):
        a = a.astype(BF16)
        for h in heads:
            acc_s[rows[h], :] += jnp.dot(a[rows[h]], v_of_head(h), preferred_element_type=F32)

    @pl.when(j == 0)
    def _():
        acc_s[...] = jnp.zeros_like(acc_s)
        m = brow.shape[0]
        tok = lax.broadcasted_iota(jnp.int32, (m, 128), 0) & (ROWS_PER_HEAD - 1)
        key = lax.broadcasted_iota(jnp.int32, (m, 128), 1)
        a, carry = _sb_tile(logits(lambda h: kn_ref[0, h]), jnp.zeros((m, 128), F32), uo, key < tok)
        carry_s[...] = carry
        accumulate(a, lambda h: vn_ref[0, h])

    def page_head(ref, h):
        return ref[0, 0, pl.ds(h, 128, stride=H_SB), :].astype(BF16)

    stage1 = [_sb_stage1(logits(lambda h, s=s: page_head(kp[s], h)), uo, None) for s in range(n_slots)]
    carry = carry_s[...]
    for s in range(n_slots):
        a, carry = _sb_stage2(*stage1[s], carry)
        accumulate(a, lambda h, s=s: page_head(vp[s], h))
    carry_s[...] = carry

    @pl.when(j == n_steps - 1)
    def _():
        o_ref[0] = acc_s[...]


def attention_sample(q8, k_new, v_new, brow, uo, cache_k, cache_v, page_table, layer, n_slots=4):
    n_seq, n_pages = page_table.shape
    n_steps = n_pages // n_slots
    m = q8.shape[1]
    page_rows = cache_k.shape[2]

    def page_spec(s):
        return pl.BlockSpec((1, 1, page_rows, 128),
                            lambda bi, j, pt: (pt[bi, n_pages - 1 - (j * n_slots + s)], layer, 0, 0))

    per_seq = lambda shape: pl.BlockSpec(shape, lambda bi, j, pt: (bi,) + (0,) * (len(shape) - 1))
    const = lambda shape: pl.BlockSpec(shape, lambda bi, j, pt: (0,) * len(shape))
    grid_spec = pltpu.PrefetchScalarGridSpec(
        num_scalar_prefetch=1,
        grid=(n_seq, n_steps),
        in_specs=[per_seq((1, m, 128)), per_seq((1, H_SB, 128, 128)), per_seq((1, H_SB, 128, 128)),
                  const((m, 128)), const((256, 256))]
                 + [page_spec(s) for s in range(n_slots)] * 2,
        out_specs=per_seq((1, m, 128)),
        scratch_shapes=[pltpu.VMEM((m, 128), F32), pltpu.VMEM((m, 128), F32)],
    )
    return pl.pallas_call(
        functools.partial(_attn_sample_kernel, n_slots=n_slots, n_steps=n_steps),
        grid_spec=grid_spec,
        out_shape=jax.ShapeDtypeStruct((n_seq, m, 128), F32),
        compiler_params=_params(32, 2),
        name="attention_sample",
    )(page_table, q8, k_new, v_new, brow, uo, *([cache_k] * n_slots), *([cache_v] * n_slots))


def _merge_kernel(x_ref, oa_ref, ob_ref, oc_ref, gates_ref, wb_ref, wo_ref, ga_ref, o_ref):
    merged = None
    for br, o_b in enumerate((oa_ref, ob_ref, oc_ref)):
        gate = jax.nn.sigmoid(gates_ref[:, br * D_MODEL:(br + 1) * D_MODEL])
        term = gate * jnp.dot(o_b[...], wb_ref[br], preferred_element_type=F32)
        merged = term if merged is None else merged + term
    y = jnp.dot(merged.astype(BF16), wo_ref[...], preferred_element_type=F32)
    o_ref[...] = x_ref[...] + ga_ref[0] * y


def merge_out(x, o_ret, o_sgu, o_sb, y, w_branch, w_out, ga, *, tm, rows_per_mod):
    t, d = x.shape
    r = ga.shape[1]
    tiles_per_mod = rows_per_mod // tm
    row = lambda w: pl.BlockSpec((tm, w), lambda i: (i, 0))
    return pl.pallas_call(
        _merge_kernel,
        grid=(t // tm,),
        in_specs=[row(d), row(512), row(512), row(512), row(N_GATE_COLS),
                  pl.BlockSpec((3, BRANCH_W, d), lambda i: (0, 0, 0)),
                  pl.BlockSpec((d, d), lambda i: (0, 0)),
                  pl.BlockSpec((1, r, d), lambda i: (i // tiles_per_mod, 0, 0))],
        out_specs=row(d),
        out_shape=jax.ShapeDtypeStruct((t, d), F32),
        compiler_params=_params(48, 1),
        name="merge_out",
    )(x, o_ret, o_sgu, o_sb, y, w_branch, w_out, ga)


def _topk_rows(problems, k, tie_safe):
    vals = [v for v, _ in problems]
    ids = [i for _, i in problems]
    ranks = [jnp.full(v.shape, NOT_RANKED, F32) for v in vals]
    tops = [[] for _ in problems]
    for a in range(k):
        ms = [jnp.max(v, axis=0, keepdims=True) for v in vals]
        if tie_safe:
            firsts = [jnp.min(jnp.where(v == m, i, 1e9), axis=0, keepdims=True) for v, m, i in zip(vals, ms, ids)]
            hits = [i == f for i, f in zip(ids, firsts)]
        else:
            hits = [v == m for v, m in zip(vals, ms)]
        ranks = [jnp.where(h, float(a), r) for h, r in zip(hits, ranks)]
        vals = [jnp.where(h, -jnp.inf, v) for h, v in zip(hits, vals)]
        for t, m in zip(tops, ms):
            t.append(m)
    taken = [jnp.sum(jnp.where(v == -jnp.inf, 1.0, 0.0), axis=0, keepdims=True) for v in vals]
    return list(zip(tops, ranks, vals, taken))


def _select_candidates(s1, s2, tie_safe):
    tt = s1.shape[1]
    key_id = lax.broadcasted_iota(jnp.int32, (PEER_NKEYS, tt), 0).astype(F32)
    (top1, rank1, _, taken1), (top2_rows, rank2, _, taken2) = _topk_rows(
        [(s1, key_id), (s2, key_id)], PEER_TOPK, tie_safe)
    top_id = lax.broadcasted_iota(jnp.int32, (PEER_TOPK, tt), 0)
    t1 = jnp.zeros((PEER_TOPK, tt), F32)
    t2 = jnp.zeros((PEER_TOPK, tt), F32)
    for r in range(PEER_TOPK):
        t1 = jnp.where(top_id == r, top1[r], t1)
        t2 = jnp.where(top_id == r, top2_rows[r], t2)
    half = PEER_TOPK // 2
    rank_id = top_id.astype(F32)
    slabs = [t1 + top2_rows[0], top1[0] + t2[half:]]
    ids = [rank_id * PEER_TOPK, rank_id[half:]]
    for b in range(1, half):
        slabs.append(t1[:half] + top2_rows[b])
        ids.append(rank_id[:half] * PEER_TOPK + float(b))
    state = dict(s1=s1, s2=s2, top1=top1, top2_rows=top2_rows, rank1=rank1, rank2=rank2, t1=t1, t2=t2,
                 taken=taken1 + taken2)
    return state, (jnp.concatenate(slabs, axis=0), jnp.concatenate(ids, axis=0))


def _select_finish(state, left, taken3):
    s1, s2, top1, top2_rows = state["s1"], state["s2"], state["top1"], state["top2_rows"]
    rank1, rank2, t1, t2 = state["rank1"], state["rank2"], state["t1"], state["t2"]
    tt = s1.shape[1]
    half = PEER_TOPK // 2
    chosen = jnp.where(left == -jnp.inf, 1.0, 0.0)
    ch_b0 = chosen[:PEER_TOPK]
    ch_a0 = chosen[PEER_TOPK:PEER_TOPK + half]
    e1_top = jnp.exp(t1 - top1[0])
    e2_top = jnp.exp(t2 - top2_rows[0])
    inner_cnt = jnp.zeros((half, tt), F32)
    inner_e2 = jnp.zeros((half, tt), F32)
    for b in range(1, half):
        ch = chosen[PEER_TOPK + b * half:PEER_TOPK + (b + 1) * half]
        inner_cnt = inner_cnt + ch
        inner_e2 = inner_e2 + ch * e2_top[b:b + 1]
    tail_cnt = jnp.sum(ch_a0, axis=0, keepdims=True)
    z = (jnp.sum(ch_b0 * e1_top, axis=0, keepdims=True) + jnp.sum(ch_a0 * e2_top[half:], axis=0, keepdims=True)
         + jnp.sum(inner_e2 * e1_top[:half], axis=0, keepdims=True))
    counts = [ch_b0[a:a + 1] + (inner_cnt[a:a + 1] if a < half else 0.0) + (tail_cnt if a == 0 else 0.0)
              for a in range(PEER_TOPK)]
    c1 = jnp.zeros((PEER_NKEYS, tt), F32)
    for a in range(PEER_TOPK):
        c1 = jnp.where(rank1 == float(a), counts[a], c1)
    f1 = jnp.exp(s1 - top1[0]) / z
    return c1, f1, rank2, jnp.exp(s2 - top2_rows[0]), state["taken"] + taken3


def _peer_select_chunks(scores, tie_safe):
    states, cands = zip(*[_select_candidates(s1, s2, tie_safe) for s1, s2 in scores])
    joint = _topk_rows(list(cands), PEER_TOPK, tie_safe)
    return [_select_finish(state, left, taken3) for state, (_, _, left, taken3) in zip(states, joint)]


def _peer_select_kernel(q_ref, sk_ref, c1_ref, f1_ref, r2_ref, f2_ref):
    tt = q_ref.shape[0]
    grouped = (PEER_NKEYS // BF16_ROWS, BF16_ROWS, LANES)
    n_chunks = tt // LANES
    pairs = []
    for c0 in range(0, n_chunks, 2):
        toks = [slice(c * LANES, (c + 1) * LANES) for c in range(c0, min(c0 + 2, n_chunks))]
        scores = [[_dot_nt(sk_ref[p], q_ref[tok, p * 128:(p + 1) * 128], precision=lax.Precision.HIGHEST)
                   for p in range(2)] for tok in toks]
        pairs.append((toks, scores))

    def write(toks, results):
        for tok, (c1, f1, r2, f2, _) in zip(toks, results):
            c1_ref[0, :, tok] = c1
            f1_ref[0, :, tok] = f1
            r2_ref[0, :, :, tok] = r2.astype(BF16).reshape(grouped)
            f2_ref[0, :, :, tok] = f2.astype(BF16).reshape(grouped)

    tied = 0.0
    for toks, scores in pairs:
        fast = _peer_select_chunks(scores, tie_safe=False)
        write(toks, fast)
        tied = tied + sum(jnp.sum(jnp.where(taken != 3.0 * PEER_TOPK, 1.0, 0.0)) for *_, taken in fast)

    @pl.when(tied > 0.0)
    def _():
        for toks, scores in pairs:
            write(toks, _peer_select_chunks(scores, tie_safe=True))


def peer_select(q, sub_keys, tt):
    t = q.shape[0]
    n_groups = PEER_NKEYS // BF16_ROWS
    shape1 = jax.ShapeDtypeStruct((PEER_HEADS, PEER_NKEYS, t), F32)
    shape2 = jax.ShapeDtypeStruct((PEER_HEADS, n_groups, BF16_ROWS, t), BF16)
    spec1 = pl.BlockSpec((1, PEER_NKEYS, tt), lambda i, h: (h, 0, i))
    spec2 = pl.BlockSpec((1, n_groups, BF16_ROWS, tt), lambda i, h: (h, 0, 0, i))
    return pl.pallas_call(
        _peer_select_kernel,
        grid=(t // tt, PEER_HEADS),
        in_specs=[pl.BlockSpec((tt, 256), lambda i, h: (i, h)),
                  pl.BlockSpec((2, PEER_NKEYS, 128), lambda i, h: (0, 0, 0))],
        out_specs=[spec1, spec1, spec2, spec2], out_shape=[shape1, shape1, shape2, shape2],
        compiler_params=_params(32, 2),
        name="peer_select",
    )(q, sub_keys)


def _gelu_tanh(x):
    k = -2.0 * 0.7978845608028654 * LOG2E
    return x / (1.0 + jnp.exp2(x * (k + (k * 0.044715) * (x * x))))


def _peer_dense_kernel(h_ref, u_ref, vt_ref, c1_ref, f1_ref, r2_ref, f2_ref, x_ref, ga_ref, o_ref,
                       acc, act, wact, *, n_rows, n_steps):
    e = pl.program_id(1)
    tt = h_ref.shape[0]

    @pl.when(e == 0)
    def _():
        acc[...] = jnp.zeros_like(acc)

    chunk = min(tt, 2 * LANES)
    zero = jnp.zeros((BF16_ROWS, chunk), BF16)
    n_groups = PEER_NKEYS // BF16_ROWS
    te = u_ref.shape[0]
    rows_per_part = n_rows // N_DENSE_PARTS
    tp = rows_per_part * PEER_NKEYS
    gp = rows_per_part * n_groups

    def project(p):
        return _dot_nt(u_ref[p * tp:(p + 1) * tp, :], h_ref[...])

    def combine(p):
        return jnp.dot(vt_ref[:, p * tp:(p + 1) * tp], wact[p * gp:(p + 1) * gp].reshape(tp, tt),
                       preferred_element_type=F32)

    raw_next = project(0)
    for p in range(N_DENSE_PARTS):
        raw = raw_next
        if p + 1 < N_DENSE_PARTS:
            raw_next = project(p + 1)
        act[p * gp:(p + 1) * gp] = _gelu_tanh(raw).astype(BF16).reshape(gp, BF16_ROWS, tt)
        if p > 0:
            acc[...] += combine(p - 1)
        for il in range(p * rows_per_part, (p + 1) * rows_per_part):
            for tc in range(tt // chunk):
                lanes = slice(tc * chunk, (tc + 1) * chunk)
                w = [zero] * n_groups
                for h in range(PEER_HEADS):
                    c1 = jnp.broadcast_to(c1_ref[h, il:il + 1, lanes], (BF16_ROWS, chunk)).astype(BF16)
                    f1 = jnp.broadcast_to(f1_ref[h, il:il + 1, lanes], (BF16_ROWS, chunk)).astype(BF16)
                    for g in range(n_groups):
                        w[g] = w[g] + jnp.where(r2_ref[h, g, :, lanes] < c1, f2_ref[h, g, :, lanes], zero) * f1
                for g in range(n_groups):
                    wact[il * n_groups + g, :, lanes] = w[g] * act[il * n_groups + g, :, lanes]
    acc[...] += combine(N_DENSE_PARTS - 1)

    @pl.when(e == n_steps - 1)
    def _():
        o_ref[...] = x_ref[...] + ga_ref[0] * acc[...].T


def peer_dense(x, h2, exp_u, exp_vt, sel, ga, *, tt, rows_per_mod, n_rows=8):
    t, d = x.shape
    n_exp = exp_u.shape[0]
    te = n_rows * PEER_NKEYS
    n_steps = n_exp // te
    r = ga.shape[1]
    tiles_per_mod = rows_per_mod // tt
    row_spec = pl.BlockSpec((PEER_HEADS, n_rows, tt), lambda i, e: (0, e, i))
    n_groups = PEER_NKEYS // BF16_ROWS
    full_spec = pl.BlockSpec((PEER_HEADS, n_groups, BF16_ROWS, tt), lambda i, e: (0, 0, 0, i))
    grouped = pltpu.VMEM((te // BF16_ROWS, BF16_ROWS, tt), BF16)
    return pl.pallas_call(
        functools.partial(_peer_dense_kernel, n_rows=n_rows, n_steps=n_steps),
        grid=(t // tt, n_steps),
        in_specs=[pl.BlockSpec((tt, d), lambda i, e: (i, 0)),
                  pl.BlockSpec((te, d), lambda i, e: (e, 0)),
                  pl.BlockSpec((d, te), lambda i, e: (0, e)),
                  row_spec, row_spec, full_spec, full_spec,
                  pl.BlockSpec((tt, d), lambda i, e: (i, 0)),
                  pl.BlockSpec((1, r, d), lambda i, e: (i // tiles_per_mod, 0, 0))],
        out_specs=pl.BlockSpec((tt, d), lambda i, e: (i, 0)),
        out_shape=jax.ShapeDtypeStruct((t, d), F32),
        scratch_shapes=[pltpu.VMEM((d, tt), F32), grouped, grouped],
        compiler_params=_params(56, 2),
        name="peer_dense",
    )(h2, exp_u, exp_vt, *sel, x, ga)


def _token_tile(rows_per_mod):
    return 512 if rows_per_mod % 512 == 0 else rows_per_mod


def _layer(x, mods, lw, *, n_batch, sample):
    t, d = x.shape
    l = t // n_batch
    sh1, sc1, ga1, sh2, sc2, ga2 = mods
    rows_per_mod = l if sample is None else t
    tm = _token_tile(rows_per_mod)
    tm_in = 2 * tm if rows_per_mod % (2 * tm) == 0 else tm
    y = norm_mod_matmul(x, lw["g_norm1"], sc1, sh1, lw["w_in"], tm=tm_in, tn=1536, rows_per_mod=rows_per_mod)

    qn, kn, knb, vb = qk_prep(y, lw["g_q"], lw["g_k"], tm)
    if sample is None:
        y3 = y.reshape(n_batch, l, -1)
        o_ret, s_new = retention_prompt(y3, lw["rope_cos"], lw["rope_sin"], lw["g_ret"])
        o_sgu, vn_rows = spatial_gating(y3, lw["g_sgu"], lw["w_mix"], lw["b_mix"], emit_vn=False)
        o_sb = attention_prompt(qn.reshape(n_batch, l, 512), knb.reshape(n_batch, l, 512),
                                vb.reshape(n_batch, l, 512), lw["b_sb"], lw["uo"])
    else:
        y3 = y.reshape(1, t, -1)
        o_ret, s_new = retention_sample(y3, lw["rope_cos"], lw["rope_sin"], lw["g_ret"],
                                        sample["state_ret"], sample["layer"], l)
        o_sgu, vn_rows = spatial_gating(y3, lw["g_sgu"], lw["w_mix"], lw["b_mix"], emit_vn=True)
        def head_major(a, rows):
            a = jnp.transpose(a.reshape(n_batch, l, H_SB, D_SB), (0, 2, 1, 3))
            return jnp.pad(a, ((0, 0), (0, 0), (0, rows - l), (0, 0)))

        q8 = head_major(qn, ROWS_PER_HEAD).reshape(n_batch, H_SB * ROWS_PER_HEAD, D_SB)
        brow = jnp.broadcast_to(jnp.repeat(lw["b_sb"] * LOG2E, ROWS_PER_HEAD)[:, None],
                                (H_SB * ROWS_PER_HEAD, 128))
        o8 = attention_sample(q8, head_major(knb, 128), head_major(vb, 128), brow, lw["uo"],
                              sample["cache_k"], sample["cache_v"], sample["page_table"], sample["layer"])
        o8 = o8.reshape(n_batch, H_SB, ROWS_PER_HEAD, D_SB)[:, :, :l]
        o_sb = jnp.transpose(o8, (0, 2, 1, 3)).reshape(t, 512).astype(BF16)

    x = merge_out(x, o_ret.reshape(t, 512), o_sgu.reshape(t, 512), o_sb.reshape(t, 512), y,
                  lw["w_branch"], lw["w_out"], ga1, tm=tm, rows_per_mod=rows_per_mod)

    pq, h2 = norm_mod_matmul(x, lw["g_norm2"], sc2, sh2, lw["w_pq"], tm=tm, tn=512,
                             rows_per_mod=rows_per_mod, emit_h=True, precise=True)
    tt = tm
    sel = peer_select(pq, lw["sub_keys"], tt)
    x = peer_dense(x, h2, lw["exp_u"], lw["exp_vt"], sel, ga2, tt=tt, rows_per_mod=rows_per_mod)
    v_raw = y[:, CB_BV * 512:(CB_BV + 1) * 512]
    return x, s_new, kn, v_raw, vn_rows


def kernel(x_prompt, x_sample, cache_k, cache_v, state_ret, page_table, c_prompt, c_sample, w_ada, b_ada,
           g_norm1, g_norm2, w_in, g_ret, g_q, g_k, b_sb, g_sgu, w_sp, b_sp, w_branch, w_out, w_pq,
           sub_keys, exp_u, exp_v):
    n_b, seq, d = x_prompt.shape
    n_s, dec, _ = x_sample.shape
    depth = w_in.shape[0]
    n_pool, _, page, _, _ = cache_k.shape
    past_len = page_table.shape[1] * page
    t_s = n_s * dec

    mod = ada_mod(jnp.concatenate([c_prompt, c_sample], axis=0), w_ada, b_ada)
    cache_k4 = cache_k.reshape(n_pool, depth, page * H_SB, D_SB)
    cache_v4 = cache_v.reshape(n_pool, depth, page * H_SB, D_SB)
    uo = _cumsum_weights()
    cos_p, sin_p = _rope_tables(jnp.arange(seq))
    cos_s, sin_s = _rope_tables(past_len + (jnp.arange(t_s) % dec))

    pos = np.arange(t_s)
    same_seq = jnp.asarray((pos[:, None] // dec) == (pos[None, :] // dec))
    tri = jnp.asarray(np.tril(np.ones((SGU_CHUNK, SGU_CHUNK), bool)))

    xp = x_prompt.reshape(n_b * seq, d)
    xs = x_sample.reshape(t_s, d)
    outs = {k: [] for k in ("kp", "vp", "ks", "vs", "rp", "rs", "us")}
    for l in range(depth):
        w_in_l = jnp.concatenate([w_in[l][:, -N_GATE_COLS:], w_in[l][:, :-N_GATE_COLS]], axis=1).astype(BF16)
        shared = dict(g_norm1=g_norm1[l], g_norm2=g_norm2[l], w_in=w_in_l, g_ret=g_ret[l], g_q=g_q[l], g_k=g_k[l],
                      b_sb=b_sb[l], g_sgu=g_sgu[l], w_branch=w_branch[l].astype(BF16), w_out=w_out[l].astype(BF16),
                      w_pq=_split_bf16(w_pq[l]), sub_keys=sub_keys[l], exp_u=exp_u[l].astype(BF16),
                      exp_vt=exp_v[l].T.astype(BF16), uo=uo)
        mods_p = [m.reshape(n_b, 1, d) for m in jnp.split(mod[l, :n_b], 6, axis=-1)]
        mods_s = [jnp.repeat(m, dec, axis=0).reshape(1, t_s, d) for m in jnp.split(mod[l, n_b:], 6, axis=-1)]

        w_tri = jnp.where(tri[None], w_sp[l], 0.0)
        lw_p = dict(shared, rope_cos=cos_p, rope_sin=sin_p, w_mix=w_tri.astype(BF16),
                    b_mix=jnp.broadcast_to(b_sp[l][:, :, None], (G_SGU, SGU_CHUNK, 128)))
        xp, rp, kp, vp, _ = _layer(xp, mods_p, lw_p, n_batch=n_b, sample=None)

        w_blk = jnp.where(same_seq[None], jnp.tile(w_tri[:, :dec, :dec], (1, n_s, n_s)), 0.0)
        b_blk = jnp.tile(b_sp[l][:, :dec], (1, n_s))
        lw_s = dict(shared, rope_cos=cos_s, rope_sin=sin_s, w_mix=w_blk.astype(BF16),
                    b_mix=jnp.broadcast_to(b_blk[:, :, None], (G_SGU, t_s, 128)))
        sample = dict(state_ret=state_ret, cache_k=cache_k4, cache_v=cache_v4, page_table=page_table, layer=l)
        xs, rs, ks, vs, us = _layer(xs, mods_s, lw_s, n_batch=n_s, sample=sample)

        outs["kp"].append(kp.reshape(n_b, seq, H_SB, D_SB))
        outs["vp"].append(vp.reshape(n_b, seq, H_SB, D_SB))
        outs["ks"].append(ks.reshape(n_s, dec, H_SB, D_SB))
        outs["vs"].append(vs.reshape(n_s, dec, H_SB, D_SB))
        outs["rp"].append(rp)
        outs["rs"].append(rs)
        outs["us"].append(us.reshape(n_s, dec, -1))
    stack = lambda name: jnp.stack(outs[name], axis=1)
    return (xp.reshape(n_b, seq, d), xs.reshape(n_s, dec, d),
            stack("kp"), stack("vp"), stack("ks"), stack("vs"), stack("rp"), stack("rs"), stack("us"))
```
